```python
import jax, jax.numpy as jnp
from jax import lax
import numpy as np

D_MODEL = 1024
BATCH = 16
SEQ = 4096
DEPTH = 2
DEC_BATCH = 1
DEC_SEQ = 16384
PAST_LEN = 128

N_BRANCH = 4
BR_W = D_MODEL // N_BRANCH
N_GROUPS = 4
GROUP_W = BR_W // N_GROUPS
CHUNK = 128
POOL_WINDOWS = (2, 4, 8, 16)
CONV_A_W = 3
CONV_D_W = 31
ALPHA = (2 * DEPTH) ** 0.25
BETA = (8 * DEPTH) ** -0.25
LN_EPS = 1e-5

N_BR_COLS = 12
IN_COLS = N_BR_COLS * BR_W + N_BRANCH * D_MODEL
SPLIT_POINTS = tuple(BR_W * i for i in range(1, N_BR_COLS + 1))

kernel_name = "hybrid_gated_parallel_encoder"


def layer_norm(x, g, b):
    xf = x.astype(jnp.float32)
    mu = jnp.mean(xf, axis=-1, keepdims=True)
    var = jnp.mean(jnp.square(xf - mu), axis=-1, keepdims=True)
    return ((xf - mu) * lax.rsqrt(var + LN_EPS) * g.astype(jnp.float32) + b.astype(jnp.float32)).astype(x.dtype)


def depthwise_conv(x, w, pad):
    c = x.shape[-1]
    return lax.conv_general_dilated(
        x, w[:, None, :].astype(x.dtype), window_strides=(1,), padding=[(pad, pad)],
        dimension_numbers=("NWC", "WIO", "NWC"), feature_group_count=c)


def short_conv_mixer(h, bg, cg, w_conv):
    return bg * depthwise_conv(cg * h, w_conv, CONV_A_W // 2)


def spatial_gating(u, v, g, b, w_s, b_s):
    bn, s, _ = v.shape
    vn = layer_norm(v, g, b).reshape(bn, s // CHUNK, CHUNK, N_GROUPS, GROUP_W)
    mixed = jnp.einsum("hpq,bnqhc->bnphc", w_s.astype(v.dtype), vn) + b_s.T[None, None, :, :, None]
    return u * mixed.reshape(bn, s, BR_W).astype(u.dtype)


def multiscale_pool(p, w_pool, scale):
    bn, s, _ = p.shape
    pos = jnp.arange(s)
    outs = []
    for gi, win in enumerate(POOL_WINDOWS):
        xg = p[..., gi * GROUP_W:(gi + 1) * GROUP_W].astype(jnp.float32)
        half = win // 2
        xp = jnp.pad(xg, ((0, 0), (half, half), (0, 0)))
        cs = jnp.concatenate([jnp.zeros_like(xp[:, :1]), jnp.cumsum(xp, axis=1)], axis=1)
        wsum = cs[:, 2 * half:2 * half + s] - cs[:, :s]
        cnt = (jnp.minimum(pos + half, s) - jnp.maximum(pos - half, 0)).astype(jnp.float32)
        pooled = (wsum / cnt[None, :, None] - xg).astype(p.dtype)
        outs.append(jnp.einsum("bsc,cd->bsd", pooled, w_pool[gi]))
    return jnp.concatenate(outs, axis=-1) * scale


def conformer_conv(a, ag, w_dw, b_dw, g, b, w_pw):
    hh = a * jax.nn.sigmoid(ag)
    hh = depthwise_conv(hh, w_dw, CONV_D_W // 2) + b_dw
    hh = jax.nn.silu(layer_norm(hh, g, b))
    return hh @ w_pw


def encoder_layer(x, w_in, conv_a, ln_v_g, ln_v_b, w_s, b_s, w_pool, pool_scale,
                  conv_d, conv_d_b, ln_d_g, ln_d_b, w_pw_d, w_br, w_out, ln_g, ln_b):
    bn, s, _ = x.shape
    proj = x @ w_in
    (a_h, a_b, a_c, a_z, b_u, b_v, b_z, c_p, c_z, d_a, d_g, d_z, gates) = jnp.split(proj, SPLIT_POINTS, axis=-1)
    y_a = short_conv_mixer(a_h, a_b, a_c, conv_a) * jax.nn.silu(a_z)
    y_b = spatial_gating(b_u, b_v, ln_v_g, ln_v_b, w_s, b_s) * jax.nn.silu(b_z)
    y_c = multiscale_pool(c_p, w_pool, pool_scale) * jax.nn.silu(c_z)
    y_d = conformer_conv(d_a, d_g, conv_d, conv_d_b, ln_d_g, ln_d_b, w_pw_d) * jax.nn.silu(d_z)
    ys = jnp.stack([y_a, y_b, y_c, y_d], axis=2)
    br = jnp.einsum("bsic,icd->bsid", ys, w_br)
    g = jax.nn.sigmoid(gates.reshape(bn, s, N_BRANCH, D_MODEL))
    merged = jnp.sum(g * br, axis=2)
    out = merged @ w_out
    return layer_norm(ALPHA * x + out, ln_g, ln_b)


def run_trunk(x, w_in, conv_a, ln_v_g, ln_v_b, w_s, b_s, w_pool, pool_scale,
              conv_d, conv_d_b, ln_d_g, ln_d_b, w_pw_d, w_br, w_out, ln_g, ln_b):
    for l in range(DEPTH):
        x = encoder_layer(x, w_in[l], conv_a[l], ln_v_g[l], ln_v_b[l], w_s[l], b_s[l], w_pool[l],
                          pool_scale[l], conv_d[l], conv_d_b[l], ln_d_g[l], ln_d_b[l], w_pw_d[l],
                          w_br[l], w_out[l], ln_g[l], ln_b[l])
    return x


def setup_inputs(seed: int = 0) -> dict:
    key = jax.random.key(seed)
    ks = jax.random.split(key, 20)
    n = jax.random.normal
    f32 = jnp.float32
    return {
        "x_prompt": n(ks[0], (BATCH, SEQ, D_MODEL), f32),
        "x_sample": n(ks[1], (DEC_BATCH, DEC_SEQ, D_MODEL), f32),
        "w_in": n(ks[2], (DEPTH, D_MODEL, IN_COLS), f32) * D_MODEL ** -0.5,
        "conv_a": n(ks[3], (DEPTH, CONV_A_W, BR_W), f32) * CONV_A_W ** -0.5,
        "ln_v_g": 1.0 + 0.02 * n(ks[4], (DEPTH, BR_W), f32),
        "ln_v_b": 0.02 * n(ks[5], (DEPTH, BR_W), f32),
        "w_s": n(ks[6], (DEPTH, N_GROUPS, CHUNK, CHUNK), f32) * 0.5 * CHUNK ** -0.5,
        "b_s": 1.0 + 0.02 * n(ks[7], (DEPTH, N_GROUPS, CHUNK), f32),
        "w_pool": n(ks[8], (DEPTH, N_GROUPS, GROUP_W, GROUP_W), f32) * GROUP_W ** -0.5,
        "pool_scale": 1.0 + 0.02 * n(ks[9], (DEPTH, BR_W), f32),
        "conv_d": n(ks[10], (DEPTH, CONV_D_W, BR_W), f32) * CONV_D_W ** -0.5,
        "conv_d_b": 0.02 * n(ks[11], (DEPTH, BR_W), f32),
        "ln_d_g": 1.0 + 0.02 * n(ks[12], (DEPTH, BR_W), f32),
        "ln_d_b": 0.02 * n(ks[13], (DEPTH, BR_W), f32),
        "w_pw_d": n(ks[14], (DEPTH, BR_W, BR_W), f32) * BR_W ** -0.5,
        "w_br": n(ks[15], (DEPTH, N_BRANCH, BR_W, D_MODEL), f32) * (BR_W ** -0.5 * BETA),
        "w_out": n(ks[16], (DEPTH, D_MODEL, D_MODEL), f32) * (D_MODEL ** -0.5 * BETA),
        "ln_g": 1.0 + 0.02 * n(ks[17], (DEPTH, D_MODEL), f32),
        "ln_b": 0.02 * n(ks[18], (DEPTH, D_MODEL), f32),
    }


def reference(x_prompt, x_sample, w_in, conv_a, ln_v_g, ln_v_b, w_s, b_s, w_pool, pool_scale,
              conv_d, conv_d_b, ln_d_g, ln_d_b, w_pw_d, w_br, w_out, ln_g, ln_b):
    y_prompt = run_trunk(x_prompt, w_in, conv_a, ln_v_g, ln_v_b, w_s, b_s, w_pool, pool_scale,
                         conv_d, conv_d_b, ln_d_g, ln_d_b, w_pw_d, w_br, w_out, ln_g, ln_b)
    y_sample = run_trunk(x_sample, w_in, conv_a, ln_v_g, ln_v_b, w_s, b_s, w_pool, pool_scale,
                         conv_d, conv_d_b, ln_d_g, ln_d_b, w_pw_d, w_br, w_out, ln_g, ln_b)
    return (y_prompt, y_sample)
```

```python
import functools

import jax
import jax.numpy as jnp
from jax import lax
from jax.experimental import pallas as pl
from jax.experimental.pallas import tpu as pltpu

D_MODEL = 1024
DEPTH = 2
N_BRANCH = 4
BR_W = D_MODEL // N_BRANCH
N_GROUPS = 4
GROUP_W = BR_W // N_GROUPS
CHUNK = 128
POOL_WINDOWS = (2, 4, 8, 16)
CONV_A_W = 3
CONV_D_W = 31
ALPHA = (2 * DEPTH) ** 0.25
LN_EPS = 1e-5
N_BR_COLS = 12

TILE = 512
HALO = 16
VMEM_LIMIT_BYTES = 56 * 1024 * 1024

(COL_A_H, COL_A_B, COL_A_C, COL_A_Z, COL_B_U, COL_B_V, COL_B_Z, COL_C_P, COL_C_Z,
 COL_D_A, COL_D_G, COL_D_Z) = range(N_BR_COLS)
GATE_COL0 = N_BR_COLS * BR_W


def _sigmoid(x):
    return 0.5 * jnp.tanh(0.5 * x) + 0.5


def _silu(x):
    return x * _sigmoid(x)


def _layer_norm(x, g, b):
    mu = jnp.mean(x, axis=-1, keepdims=True)
    xc = x - mu
    var = jnp.mean(xc * xc, axis=-1, keepdims=True)
    return xc * lax.rsqrt(var + LN_EPS) * g + b


def _dot(a, b):
    return jnp.dot(a, b, preferred_element_type=jnp.float32)


def _layer_kernel(seq_len, x_ref, xp_ref, xn_ref, w_in_ref, conv_a_ref, ln_v_g_ref, ln_v_b_ref,
                  w_s_ref, b_s_ref, w_pool_ref, pool_scale_ref, conv_d_ref, conv_d_b_ref,
                  ln_d_g_ref, ln_d_b_ref, w_pw_d_ref, w_br_ref, w_out_ref, ln_g_ref, ln_b_ref,
                  o_ref):
    t = pl.program_id(1)
    n_t = pl.num_programs(1)
    bf16 = jnp.bfloat16
    f32 = jnp.float32

    x = x_ref[0]
    xp = jnp.where(t > 0, xp_ref[0], 0.0)
    xn = jnp.where(t < n_t - 1, xn_ref[0], 0.0)
    x_bf = x.astype(bf16)
    xh_bf = jnp.concatenate([xp.astype(bf16), x_bf, xn.astype(bf16)], axis=0)

    def proj(lhs, col):
        return _dot(lhs, w_in_ref[:, col * BR_W:(col + 1) * BR_W])

    def centre(v, shift=0):
        return v[HALO + shift:HALO + shift + TILE]

    u_a = proj(xh_bf, COL_A_C) * proj(xh_bf, COL_A_H)
    conv_a = conv_a_ref[...]
    acc_a = centre(u_a, -1) * conv_a[0:1]
    for k in range(1, CONV_A_W):
        acc_a = acc_a + centre(u_a, k - CONV_A_W // 2) * conv_a[k:k + 1]
    y_a = proj(x_bf, COL_A_B) * acc_a * _silu(proj(x_bf, COL_A_Z))

    vn = _layer_norm(proj(x_bf, COL_B_V), ln_v_g_ref[...], ln_v_b_ref[...]).astype(bf16)
    w_s = w_s_ref[...]
    b_s = b_s_ref[...]
    lane = lax.broadcasted_iota(jnp.int32, (CHUNK, 2 * GROUP_W), 1)
    lo_half = lane < GROUP_W
    mixed_chunks = []
    for c in range(TILE // CHUNK):
        r = _dot(w_s, vn[c * CHUNK:(c + 1) * CHUNK])
        left = jnp.where(lo_half, r[0:CHUNK, 0:2 * GROUP_W], r[CHUNK:2 * CHUNK, 0:2 * GROUP_W])
        right = jnp.where(lo_half, r[2 * CHUNK:3 * CHUNK, 2 * GROUP_W:], r[3 * CHUNK:, 2 * GROUP_W:])
        mixed_chunks.append(jnp.concatenate([left, right], axis=1) + b_s)
    mixed = jnp.concatenate(mixed_chunks, axis=0)
    y_b = proj(x_bf, COL_B_U) * mixed * _silu(proj(x_bf, COL_B_Z))

    p = proj(xh_bf, COL_C_P)
    n_h = TILE + 2 * HALO

    def shifted(v, s):
        z = jnp.zeros((abs(s), v.shape[1]), v.dtype)
        return jnp.concatenate([z, v[:n_h - abs(s)]], axis=0) if s < 0 else jnp.concatenate([v[s:], z], axis=0)

    w2 = shifted(p, -1) + p
    w4 = shifted(w2, -1) + shifted(w2, 1)
    w8 = shifted(w4, -2) + shifted(w4, 2)
    w16 = shifted(w8, -4) + shifted(w8, 4)
    lane_c = lax.broadcasted_iota(jnp.int32, (TILE, BR_W), 1)
    grp = lane_c // GROUP_W
    wsum = jnp.where(grp == 0, centre(w2), jnp.where(grp == 1, centre(w4),
                     jnp.where(grp == 2, centre(w8), centre(w16))))
    half = jnp.where(grp == 0, POOL_WINDOWS[0] // 2, jnp.where(grp == 1, POOL_WINDOWS[1] // 2,
                     jnp.where(grp == 2, POOL_WINDOWS[2] // 2, POOL_WINDOWS[3] // 2)))
    pos = t * TILE + lax.broadcasted_iota(jnp.int32, (TILE, BR_W), 0)
    cnt = (jnp.minimum(pos + half, seq_len) - jnp.maximum(pos - half, 0)).astype(f32)
    pooled = (wsum / cnt - centre(p)).astype(bf16)
    y_c = _dot(pooled, w_pool_ref[...]) * pool_scale_ref[...] * _silu(proj(x_bf, COL_C_Z))

    hh = proj(xh_bf, COL_D_A) * _sigmoid(proj(xh_bf, COL_D_G))
    conv_d = conv_d_ref[...]
    acc_d = centre(hh, -(CONV_D_W // 2)) * conv_d[0:1]
    for k in range(1, CONV_D_W):
        acc_d = acc_d + centre(hh, k - CONV_D_W // 2) * conv_d[k:k + 1]
    acc_d = acc_d + conv_d_b_ref[...]
    hd = _silu(_layer_norm(acc_d, ln_d_g_ref[...], ln_d_b_ref[...])).astype(bf16)
    y_d = _dot(hd, w_pw_d_ref[...]) * _silu(proj(x_bf, COL_D_Z))

    merged = None
    for i, y in enumerate((y_a, y_b, y_c, y_d)):
        br = _dot(y.astype(bf16), w_br_ref[i])
        g = _sigmoid(_dot(x_bf, w_in_ref[:, GATE_COL0 + i * D_MODEL:GATE_COL0 + (i + 1) * D_MODEL]))
        merged = g * br if merged is None else merged + g * br
    out = _dot(merged.astype(bf16), w_out_ref[...])
    o_ref[0] = _layer_norm(ALPHA * x + out, ln_g_ref[...], ln_b_ref[...])


def _resident(shape):
    return pl.BlockSpec(shape, lambda b, t: (0,) * len(shape), pipeline_mode=pl.Buffered(1))


def _encoder_layer(x, weights):
    bn, s, d = x.shape
    assert d == D_MODEL and s % TILE == 0 and TILE % CHUNK == 0 and TILE % HALO == 0
    assert HALO % 8 == 0 and HALO >= CONV_D_W // 2 and HALO >= max(POOL_WINDOWS) // 2
    n_t = s // TILE
    halo_per_tile = TILE // HALO
    n_halo_blocks = s // HALO
    in_specs = [
        pl.BlockSpec((1, TILE, d), lambda b, t: (b, t, 0)),
        pl.BlockSpec((1, HALO, d), lambda b, t: (b, jnp.maximum(t * halo_per_tile - 1, 0), 0)),
        pl.BlockSpec((1, HALO, d), lambda b, t: (b, jnp.minimum((t + 1) * halo_per_tile, n_halo_blocks - 1), 0)),
    ] + [_resident(w.shape) for w in weights]
    return pl.pallas_call(
        functools.partial(_layer_kernel, s),
        grid=(bn, n_t),
        in_specs=in_specs,
        out_specs=pl.BlockSpec((1, TILE, d), lambda b, t: (b, t, 0)),
        out_shape=jax.ShapeDtypeStruct(x.shape, x.dtype),
        compiler_params=pltpu.CompilerParams(
            dimension_semantics=("parallel", "parallel"),
            vmem_limit_bytes=VMEM_LIMIT_BYTES),
        name="encoder_layer",
    )(x, x, x, *weights)


def _prepare_layer_weights(l, w_in, conv_a, ln_v_g, ln_v_b, w_s, b_s, w_pool, pool_scale,
                           conv_d, conv_d_b, ln_d_g, ln_d_b, w_pw_d, w_br, w_out, ln_g, ln_b):
    bf16 = jnp.bfloat16
    row = lambda v: v[l].reshape(1, -1)
    b_s_full = jnp.repeat(b_s[l].T, GROUP_W, axis=1)
    w_pool_bd = jax.scipy.linalg.block_diag(*[w_pool[l, g] for g in range(N_GROUPS)])
    return (
        w_in[l].astype(bf16), conv_a[l], row(ln_v_g), row(ln_v_b),
        w_s[l].reshape(N_GROUPS * CHUNK, CHUNK).astype(bf16), b_s_full,
        w_pool_bd.astype(bf16), row(pool_scale), conv_d[l], row(conv_d_b), row(ln_d_g), row(ln_d_b),
        w_pw_d[l].astype(bf16), w_br[l].astype(bf16), w_out[l].astype(bf16), row(ln_g), row(ln_b),
    )


def kernel(x_prompt, x_sample, w_in, conv_a, ln_v_g, ln_v_b, w_s, b_s, w_pool, pool_scale, conv_d, conv_d_b, ln_d_g, ln_d_b, w_pw_d, w_br, w_out, ln_g, ln_b):
    params = (w_in, conv_a, ln_v_g, ln_v_b, w_s, b_s, w_pool, pool_scale, conv_d, conv_d_b,
              ln_d_g, ln_d_b, w_pw_d, w_br, w_out, ln_g, ln_b)
    y_prompt, y_sample = x_prompt, x_sample
    for l in range(DEPTH):
        weights = _prepare_layer_weights(l, *params)
        y_prompt = _encoder_layer(y_prompt, weights)
        y_sample = _encoder_layer(y_sample, weights)
    return (y_prompt, y_sample)
```

```python
import functools

import jax
import jax.numpy as jnp
from jax import lax
from jax.experimental import pallas as pl
from jax.experimental.pallas import tpu as pltpu

D_MODEL = 1024
DEPTH = 2
N_BRANCH = 4
BR_W = D_MODEL // N_BRANCH
N_GROUPS = 4
GROUP_W = BR_W // N_GROUPS
CHUNK = 128
POOL_WINDOWS = (2, 4, 8, 16)
CONV_A_W = 3
CONV_D_W = 31
ALPHA = (2 * DEPTH) ** 0.25
LN_EPS = 1e-5
N_BR_COLS = 12

LANES = 128
SUBLANES = 8
N_HALF = BR_W // LANES
TILE = 512
HALO = 16
PHASES = 4
PHASE_ROWS = 64
OUT_ROWS = 256
VMEM_LIMIT_BYTES = 56 * 1024 * 1024

(COL_A_H, COL_A_B, COL_A_C, COL_A_Z, COL_B_U, COL_B_V, COL_B_Z, COL_C_P, COL_C_Z,
 COL_D_A, COL_D_G, COL_D_Z) = range(N_BR_COLS)
GATE_COL0 = N_BR_COLS * BR_W
HALF_SCALED_COLS = (COL_A_Z, COL_B_Z, COL_C_Z, COL_D_Z, COL_D_A, COL_D_G)


def _gate(h):
    return jnp.tanh(h) + 1.0


def _half_silu(h):
    return h * _gate(h)


def _layer_norm(x, g, b):
    mu = jnp.mean(x, axis=-1, keepdims=True)
    xc = x - mu
    var = jnp.mean(xc * xc, axis=-1, keepdims=True)
    return xc * lax.rsqrt(var + LN_EPS) * g + b


def _dot(a, b):
    return jnp.dot(a, b, preferred_element_type=jnp.float32)


def _phase_blocks():
    return [tau + PHASES * m0 for tau in range(PHASES) for m0 in range(0, TILE // PHASES, PHASE_ROWS)]


def _phase_rows(ref, row0):
    return ref[pl.ds(row0, PHASE_ROWS, stride=PHASES), :]


def _store_halves(refs, v):
    for h, ref in enumerate(refs):
        ref[...] = v[:, h * LANES:(h + 1) * LANES]


def _load_halves(refs):
    return jnp.concatenate([ref[...] for ref in refs], axis=1)


def _layer_kernel(seq_len, x_ref, xp_ref, xn_ref, w_in_ref, conv_a_ref, ln_v_g_ref, ln_v_b_ref,
                  w_s_ref, b_s_ref, w_pool_ref, pool_scale_ref, conv_d_ref, conv_d_b_ref,
                  ln_d_g_ref, ln_d_b_ref, w_pw_d_ref, w_br_ref, w_out_ref, ln_g_ref, ln_b_ref,
                  o_ref,
                  hh0_ref, hh1_ref, ua0_ref, ua1_ref, p0_ref, p1_ref,
                  cd0_ref, cd1_ref, ca0_ref, ca1_ref, pl0_ref, pl1_ref):
    hh_refs, ua_refs, p_refs = (hh0_ref, hh1_ref), (ua0_ref, ua1_ref), (p0_ref, p1_ref)
    cd_refs, ca_refs, pool_refs = (cd0_ref, cd1_ref), (ca0_ref, ca1_ref), (pl0_ref, pl1_ref)
    t = pl.program_id(1)
    n_t = pl.num_programs(1)
    bf16 = jnp.bfloat16
    f32 = jnp.float32

    x = x_ref[0]
    xp = jnp.where(t > 0, xp_ref[0], 0.0)
    xn = jnp.where(t < n_t - 1, xn_ref[0], 0.0)
    x_bf = x.astype(bf16)
    xh_bf = jnp.concatenate([xp.astype(bf16), x_bf, xn.astype(bf16)], axis=0)

    def proj(lhs, col):
        return _dot(lhs, w_in_ref[:, col * BR_W:(col + 1) * BR_W])

    def gate_proj(i):
        return _dot(x_bf, w_in_ref[:, GATE_COL0 + i * D_MODEL:GATE_COL0 + (i + 1) * D_MODEL])

    _store_halves(hh_refs, proj(xh_bf, COL_D_A) * _gate(proj(xh_bf, COL_D_G)))
    _store_halves(ua_refs, proj(xh_bf, COL_A_C) * proj(xh_bf, COL_A_H))
    _store_halves(p_refs, proj(xh_bf, COL_C_P))

    conv_a = conv_a_ref[...]
    for h in range(N_HALF):
        for row0 in _phase_blocks():
            acc = None
            for k in range(CONV_A_W):
                term = _phase_rows(ua_refs[h], HALO - CONV_A_W // 2 + k + row0) * conv_a[k:k + 1, h * LANES:(h + 1) * LANES]
                acc = term if acc is None else acc + term
            ca_refs[h][pl.ds(row0, PHASE_ROWS, stride=PHASES), :] = acc
    y_a = proj(x_bf, COL_A_B) * _load_halves(ca_refs) * _half_silu(proj(x_bf, COL_A_Z))
    merged = _gate(gate_proj(0)) * _dot(y_a.astype(bf16), w_br_ref[0])

    lo_group = lax.broadcasted_iota(jnp.int32, (PHASE_ROWS, LANES), 1) < GROUP_W
    row_step = PHASES * lax.broadcasted_iota(jnp.int32, (PHASE_ROWS, LANES), 0)
    for h in range(N_HALF):
        half_lo, half_hi = POOL_WINDOWS[2 * h] // 2, POOL_WINDOWS[2 * h + 1] // 2
        half = jnp.where(lo_group, half_lo, half_hi)
        for row0 in _phase_blocks():
            def window(d0, d1):
                s = None
                for d in range(d0, d1):
                    v = _phase_rows(p_refs[h], HALO + d + row0)
                    s = v if s is None else s + v
                return s
            inner = window(-half_lo, half_lo)
            outer = inner + window(-half_hi, -half_lo) + window(half_lo, half_hi)
            pos = t * TILE + row0 + row_step
            cnt = (jnp.minimum(pos + half, seq_len) - jnp.maximum(pos - half, 0)).astype(f32)
            pooled = jnp.where(lo_group, inner, outer) / cnt - _phase_rows(p_refs[h], HALO + row0)
            pool_refs[h][pl.ds(row0, PHASE_ROWS, stride=PHASES), :] = pooled
    pooled = _load_halves(pool_refs).astype(bf16)
    y_c = _dot(pooled, w_pool_ref[...]) * pool_scale_ref[...] * _half_silu(proj(x_bf, COL_C_Z))
    merged = merged + _gate(gate_proj(2)) * _dot(y_c.astype(bf16), w_br_ref[2])

    conv_d = conv_d_ref[...]
    conv_d_b = conv_d_b_ref[...]

    def conv_d_half(h):
        for row0 in _phase_blocks():
            acc = None
            for k in range(CONV_D_W):
                term = _phase_rows(hh_refs[h], HALO - CONV_D_W // 2 + k + row0) * conv_d[k:k + 1, h * LANES:(h + 1) * LANES]
                acc = term if acc is None else acc + term
            cd_refs[h][pl.ds(row0, PHASE_ROWS, stride=PHASES), :] = acc + conv_d_b[:, h * LANES:(h + 1) * LANES]

    conv_d_half(0)

    vn = _layer_norm(proj(x_bf, COL_B_V), ln_v_g_ref[...], ln_v_b_ref[...]).astype(bf16)
    w_s = w_s_ref[...]
    b_s = b_s_ref[...]
    lo_half = lax.broadcasted_iota(jnp.int32, (CHUNK, 2 * GROUP_W), 1) < GROUP_W
    mixed_chunks = []
    for c in range(TILE // CHUNK):
        r = _dot(w_s, vn[c * CHUNK:(c + 1) * CHUNK])
        left = jnp.where(lo_half, r[0:CHUNK, 0:2 * GROUP_W], r[CHUNK:2 * CHUNK, 0:2 * GROUP_W])
        right = jnp.where(lo_half, r[2 * CHUNK:3 * CHUNK, 2 * GROUP_W:], r[3 * CHUNK:, 2 * GROUP_W:])
        mixed_chunks.append(jnp.concatenate([left, right], axis=1) + b_s)
    mixed = jnp.concatenate(mixed_chunks, axis=0)
    y_b = proj(x_bf, COL_B_U) * mixed * _half_silu(proj(x_bf, COL_B_Z))
    merged = merged + _gate(gate_proj(1)) * _dot(y_b.astype(bf16), w_br_ref[1])

    g_d = _gate(gate_proj(3))
    conv_d_half(1)
    hd = _half_silu(_layer_norm(_load_halves(cd_refs), ln_d_g_ref[...], ln_d_b_ref[...])).astype(bf16)
    y_d = _dot(hd, w_pw_d_ref[...]) * _half_silu(proj(x_bf, COL_D_Z))
    merged = (merged + g_d * _dot(y_d.astype(bf16), w_br_ref[3])).astype(bf16)

    ln_g = ln_g_ref[...]
    ln_b = ln_b_ref[...]
    for r0 in range(0, TILE, OUT_ROWS):
        out = _dot(merged[r0:r0 + OUT_ROWS], w_out_ref[...])
        o_ref[0, r0:r0 + OUT_ROWS] = _layer_norm(ALPHA * x[r0:r0 + OUT_ROWS] + out, ln_g, ln_b)


def _resident(shape):
    return pl.BlockSpec(shape, lambda b, t: (0,) * len(shape), pipeline_mode=pl.Buffered(1))


def _encoder_layer(x, weights):
    bn, s, d = x.shape
    assert d == D_MODEL and s % TILE == 0 and TILE % CHUNK == 0 and TILE % HALO == 0
    assert HALO % SUBLANES == 0 and HALO >= CONV_D_W // 2 and HALO >= max(POOL_WINDOWS) // 2
    assert TILE % (PHASES * PHASE_ROWS) == 0 and TILE % OUT_ROWS == 0 and N_HALF * 2 == N_GROUPS
    n_t = s // TILE
    halo_per_tile = TILE // HALO
    n_halo_blocks = s // HALO
    in_specs = [
        pl.BlockSpec((1, TILE, d), lambda b, t: (b, t, 0)),
        pl.BlockSpec((1, HALO, d), lambda b, t: (b, jnp.maximum(t * halo_per_tile - 1, 0), 0)),
        pl.BlockSpec((1, HALO, d), lambda b, t: (b, jnp.minimum((t + 1) * halo_per_tile, n_halo_blocks - 1), 0)),
    ] + [_resident(w.shape) for w in weights]
    halo_scratch = pltpu.VMEM((TILE + 2 * HALO, LANES), jnp.float32)
    tile_scratch = pltpu.VMEM((TILE, LANES), jnp.float32)
    return pl.pallas_call(
        functools.partial(_layer_kernel, s),
        grid=(bn, n_t),
        in_specs=in_specs,
        out_specs=pl.BlockSpec((1, TILE, d), lambda b, t: (b, t, 0)),
        out_shape=jax.ShapeDtypeStruct(x.shape, x.dtype),
        scratch_shapes=[halo_scratch] * (3 * N_HALF) + [tile_scratch] * (3 * N_HALF),
        compiler_params=pltpu.CompilerParams(
            dimension_semantics=("parallel", "parallel"),
            vmem_limit_bytes=VMEM_LIMIT_BYTES),
        name="encoder_layer",
    )(x, x, x, *weights)


def _prepare_layer_weights(l, w_in, conv_a, ln_v_g, ln_v_b, w_s, b_s, w_pool, pool_scale,
                           conv_d, conv_d_b, ln_d_g, ln_d_b, w_pw_d, w_br, w_out, ln_g, ln_b):
    bf16 = jnp.bfloat16
    row = lambda v: v[l].reshape(1, -1)
    col_scale = jnp.ones((N_BR_COLS,), jnp.float32).at[jnp.array(HALF_SCALED_COLS)].set(0.5)
    col_scale = jnp.concatenate([jnp.repeat(col_scale, BR_W), jnp.full((N_BRANCH * D_MODEL,), 0.5, jnp.float32)])
    w_in_l = w_in[l] * col_scale[None, :]
    b_s_full = jnp.repeat(b_s[l].T, GROUP_W, axis=1)
    w_pool_bd = jax.scipy.linalg.block_diag(*[w_pool[l, g] for g in range(N_GROUPS)])
    return (
        w_in_l.astype(bf16), conv_a[l], row(ln_v_g), row(ln_v_b),
        w_s[l].reshape(N_GROUPS * CHUNK, CHUNK).astype(bf16), b_s_full,
        w_pool_bd.astype(bf16), row(pool_scale), conv_d[l], row(conv_d_b),
        0.5 * row(ln_d_g), 0.5 * row(ln_d_b),
        w_pw_d[l].astype(bf16), (0.5 * w_br[l]).astype(bf16), w_out[l].astype(bf16), row(ln_g), row(ln_b),
    )


def kernel(x_prompt, x_sample, w_in, conv_a, ln_v_g, ln_v_b, w_s, b_s, w_pool, pool_scale, conv_d, conv_d_b, ln_d_g, ln_d_b, w_pw_d, w_br, w_out, ln_g, ln_b):
    params = (w_in, conv_a, ln_v_g, ln_v_b, w_s, b_s, w_pool, pool_scale, conv_d, conv_d_b,
              ln_d_g, ln_d_b, w_pw_d, w_br, w_out, ln_g, ln_b)
    y_prompt, y_sample = x_prompt, x_sample
    for l in range(DEPTH):
        weights = _prepare_layer_weights(l, *params)
        y_prompt = _encoder_layer(y_prompt, weights)
        y_sample = _encoder_layer(y_sample, weights)
    return (y_prompt, y_sample)
```

```python
import functools

import jax
import jax.numpy as jnp
from jax import lax
from jax.experimental import pallas as pl
from jax.experimental.pallas import tpu as pltpu

D_MODEL = 1024
DEPTH = 2
N_BRANCH = 4
BR_W = D_MODEL // N_BRANCH
N_GROUPS = 4
GROUP_W = BR_W // N_GROUPS
CHUNK = 128
POOL_WINDOWS = (2, 4, 8, 16)
CONV_A_W = 3
CONV_D_W = 31
ALPHA = (2 * DEPTH) ** 0.25
LN_EPS = 1e-5
N_BR_COLS = 12

LANES = 128
SUBLANES = 8
N_HALF = BR_W // LANES
TILE = 512
HALO = 16
PHASES = 4
PHASE_ROWS = 64
OUT_ROWS = 256
VMEM_LIMIT_BYTES = 56 * 1024 * 1024

(COL_A_H, COL_A_B, COL_A_C, COL_A_Z, COL_B_U, COL_B_V, COL_B_Z, COL_C_P, COL_C_Z,
 COL_D_A, COL_D_G, COL_D_Z) = range(N_BR_COLS)
HALO_COLS = (COL_A_H, COL_A_C, COL_C_P, COL_D_A, COL_D_G, COL_A_B)
TILE_COLS = (COL_A_Z, COL_B_U, COL_B_V, COL_B_Z, COL_C_Z, COL_D_Z)
GATE_COL0 = N_BR_COLS * BR_W
HALF_SCALED_COLS = (COL_A_Z, COL_B_Z, COL_C_Z, COL_D_Z, COL_D_A, COL_D_G)


def _gate(h):
    return jnp.tanh(h) + 1.0


def _half_silu(h):
    return h * _gate(h)


def _layer_norm(x, g, b):
    mu = jnp.mean(x, axis=-1, keepdims=True)
    xc = x - mu
    var = jnp.mean(xc * xc, axis=-1, keepdims=True)
    return xc * lax.rsqrt(var + LN_EPS) * g + b


def _dot(a, b):
    return jnp.dot(a, b, preferred_element_type=jnp.float32)


def _phase_blocks():
    return [tau + PHASES * m0 for tau in range(PHASES) for m0 in range(0, TILE // PHASES, PHASE_ROWS)]


def _phase_rows(ref, row0):
    return ref[pl.ds(row0, PHASE_ROWS, stride=PHASES), :]


def _store_halves(refs, v):
    for h, ref in enumerate(refs):
        ref[...] = v[:, h * LANES:(h + 1) * LANES]


def _load_halves(refs):
    return jnp.concatenate([ref[...] for ref in refs], axis=1)


def _layer_kernel(seq_len, x_ref, xp_ref, xn_ref, w_in_ref, conv_a_ref, ln_v_g_ref, ln_v_b_ref,
                  w_s_ref, b_s_ref, w_pool_ref, pool_scale_ref, conv_d_ref, conv_d_b_ref,
                  ln_d_g_ref, ln_d_b_ref, w_pw_d_ref, w_br_ref, w_out_ref, ln_g_ref, ln_b_ref,
                  o_ref,
                  hh0_ref, hh1_ref, ua0_ref, ua1_ref, p0_ref, p1_ref,
                  cd0_ref, cd1_ref, ca0_ref, ca1_ref, pl0_ref, pl1_ref):
    hh_refs, ua_refs, p_refs = (hh0_ref, hh1_ref), (ua0_ref, ua1_ref), (p0_ref, p1_ref)
    cd_refs, ca_refs, pool_refs = (cd0_ref, cd1_ref), (ca0_ref, ca1_ref), (pl0_ref, pl1_ref)
    t = pl.program_id(1)
    n_t = pl.num_programs(1)
    bf16 = jnp.bfloat16
    f32 = jnp.float32

    x = x_ref[0]
    xp = jnp.where(t > 0, xp_ref[0], 0.0)
    xn = jnp.where(t < n_t - 1, xn_ref[0], 0.0)
    x_bf = x.astype(bf16)
    xh_bf = jnp.concatenate([xp.astype(bf16), x_bf, xn.astype(bf16)], axis=0)

    def block(v, cols, col):
        i = cols.index(col)
        return v[:, i * BR_W:(i + 1) * BR_W]

    def gate_proj(i):
        return _dot(x_bf, w_in_ref[:, GATE_COL0 + i * D_MODEL:GATE_COL0 + (i + 1) * D_MODEL])

    n_halo_cols = len(HALO_COLS) * BR_W
    ph = _dot(xh_bf, w_in_ref[:, 0:n_halo_cols])
    _store_halves(hh_refs, block(ph, HALO_COLS, COL_D_A) * _gate(block(ph, HALO_COLS, COL_D_G)))
    _store_halves(ua_refs, block(ph, HALO_COLS, COL_A_C) * block(ph, HALO_COLS, COL_A_H))
    _store_halves(p_refs, block(ph, HALO_COLS, COL_C_P))
    a_b = block(ph, HALO_COLS, COL_A_B)[HALO:HALO + TILE]
    pt = _dot(x_bf, w_in_ref[:, n_halo_cols:GATE_COL0])
    gates = [_gate(gate_proj(i)) for i in range(N_BRANCH)]

    conv_d = conv_d_ref[...]
    conv_d_b = conv_d_b_ref[...]
    for h in range(N_HALF):
        w_h = conv_d[:, h * LANES:(h + 1) * LANES]
        for m0 in range(0, TILE, PHASES * PHASE_ROWS):
            acc = [None] * PHASES
            for off in range(CONV_D_W + PHASES - 1):
                v = _phase_rows(hh_refs[h], HALO - CONV_D_W // 2 + off + m0)
                for tau in range(PHASES):
                    k = off - tau
                    if 0 <= k < CONV_D_W:
                        term = v * w_h[k:k + 1]
                        acc[tau] = term if acc[tau] is None else acc[tau] + term
            for tau in range(PHASES):
                cd_refs[h][pl.ds(tau + m0, PHASE_ROWS, stride=PHASES), :] = acc[tau] + conv_d_b[:, h * LANES:(h + 1) * LANES]

    conv_a = conv_a_ref[...]
    for h in range(N_HALF):
        for row0 in _phase_blocks():
            acc = None
            for k in range(CONV_A_W):
                term = _phase_rows(ua_refs[h], HALO - CONV_A_W // 2 + k + row0) * conv_a[k:k + 1, h * LANES:(h + 1) * LANES]
                acc = term if acc is None else acc + term
            ca_refs[h][pl.ds(row0, PHASE_ROWS, stride=PHASES), :] = acc
    y_a = a_b * _load_halves(ca_refs) * _half_silu(block(pt, TILE_COLS, COL_A_Z))

    lo_group = lax.broadcasted_iota(jnp.int32, (PHASE_ROWS, LANES), 1) < GROUP_W
    row_step = PHASES * lax.broadcasted_iota(jnp.int32, (PHASE_ROWS, LANES), 0)
    for h in range(N_HALF):
        half_lo, half_hi = POOL_WINDOWS[2 * h] // 2, POOL_WINDOWS[2 * h + 1] // 2
        half = jnp.where(lo_group, half_lo, half_hi)
        for row0 in _phase_blocks():
            def window(d0, d1):
                s = None
                for d in range(d0, d1):
                    v = _phase_rows(p_refs[h], HALO + d + row0)
                    s = v if s is None else s + v
                return s
            inner = window(-half_lo, half_lo)
            outer = inner + window(-half_hi, -half_lo) + window(half_lo, half_hi)
            pos = t * TILE + row0 + row_step
            cnt = (jnp.minimum(pos + half, seq_len) - jnp.maximum(pos - half, 0)).astype(f32)
            pooled = jnp.where(lo_group, inner, outer) / cnt - _phase_rows(p_refs[h], HALO + row0)
            pool_refs[h][pl.ds(row0, PHASE_ROWS, stride=PHASES), :] = pooled
    pooled = _load_halves(pool_refs).astype(bf16)
    y_c = _dot(pooled, w_pool_ref[...]) * pool_scale_ref[...] * _half_silu(block(pt, TILE_COLS, COL_C_Z))

    vn = _layer_norm(block(pt, TILE_COLS, COL_B_V), ln_v_g_ref[...], ln_v_b_ref[...]).astype(bf16)
    n_chunks = TILE // CHUNK
    vn_wide = jnp.concatenate([vn[c * CHUNK:(c + 1) * CHUNK] for c in range(n_chunks)], axis=1)
    r = _dot(w_s_ref[...], vn_wide)
    b_s = b_s_ref[...]
    lo_half = lax.broadcasted_iota(jnp.int32, (CHUNK, 2 * GROUP_W), 1) < GROUP_W
    mixed_chunks = []
    for c in range(n_chunks):
        c0 = c * BR_W
        left = jnp.where(lo_half, r[0:CHUNK, c0:c0 + 2 * GROUP_W], r[CHUNK:2 * CHUNK, c0:c0 + 2 * GROUP_W])
        right = jnp.where(lo_half, r[2 * CHUNK:3 * CHUNK, c0 + 2 * GROUP_W:c0 + BR_W], r[3 * CHUNK:, c0 + 2 * GROUP_W:c0 + BR_W])
        mixed_chunks.append(jnp.concatenate([left, right], axis=1) + b_s)
    mixed = jnp.concatenate(mixed_chunks, axis=0)
    y_b = block(pt, TILE_COLS, COL_B_U) * mixed * _half_silu(block(pt, TILE_COLS, COL_B_Z))

    hd = _half_silu(_layer_norm(_load_halves(cd_refs), ln_d_g_ref[...], ln_d_b_ref[...])).astype(bf16)
    y_d = _dot(hd, w_pw_d_ref[...]) * _half_silu(block(pt, TILE_COLS, COL_D_Z))

    merged = None
    for i, y in enumerate((y_a, y_b, y_c, y_d)):
        term = gates[i] * _dot(y.astype(bf16), w_br_ref[i])
        merged = term if merged is None else merged + term
    merged = merged.astype(bf16)

    ln_g = ln_g_ref[...]
    ln_b = ln_b_ref[...]
    for r0 in range(0, TILE, OUT_ROWS):
        out = _dot(merged[r0:r0 + OUT_ROWS], w_out_ref[...])
        o_ref[0, r0:r0 + OUT_ROWS] = _layer_norm(ALPHA * x[r0:r0 + OUT_ROWS] + out, ln_g, ln_b)


def _resident(shape):
    return pl.BlockSpec(shape, lambda b, t: (0,) * len(shape), pipeline_mode=pl.Buffered(1))


def _encoder_layer(x, weights):
    bn, s, d = x.shape
    assert d == D_MODEL and s % TILE == 0 and TILE % CHUNK == 0 and TILE % HALO == 0
    assert HALO % SUBLANES == 0 and HALO >= CONV_D_W // 2 and HALO >= max(POOL_WINDOWS) // 2
    assert TILE % (PHASES * PHASE_ROWS) == 0 and TILE % OUT_ROWS == 0 and N_HALF * 2 == N_GROUPS
    n_t = s // TILE
    halo_per_tile = TILE // HALO
    n_halo_blocks = s // HALO
    in_specs = [
        pl.BlockSpec((1, TILE, d), lambda b, t: (b, t, 0)),
        pl.BlockSpec((1, HALO, d), lambda b, t: (b, jnp.maximum(t * halo_per_tile - 1, 0), 0)),
        pl.BlockSpec((1, HALO, d), lambda b, t: (b, jnp.minimum((t + 1) * halo_per_tile, n_halo_blocks - 1), 0)),
    ] + [_resident(w.shape) for w in weights]
    halo_scratch = pltpu.VMEM((TILE + 2 * HALO, LANES), jnp.float32)
    tile_scratch = pltpu.VMEM((TILE, LANES), jnp.float32)
    return pl.pallas_call(
        functools.partial(_layer_kernel, s),
        grid=(bn, n_t),
        in_specs=in_specs,
        out_specs=pl.BlockSpec((1, TILE, d), lambda b, t: (b, t, 0)),
        out_shape=jax.ShapeDtypeStruct(x.shape, x.dtype),
        scratch_shapes=[halo_scratch] * (3 * N_HALF) + [tile_scratch] * (3 * N_HALF),
        compiler_params=pltpu.CompilerParams(
            dimension_semantics=("parallel", "parallel"),
            vmem_limit_bytes=VMEM_LIMIT_BYTES),
        name="encoder_layer",
    )(x, x, x, *weights)


def _prepare_layer_weights(l, w_in, conv_a, ln_v_g, ln_v_b, w_s, b_s, w_pool, pool_scale,
                           conv_d, conv_d_b, ln_d_g, ln_d_b, w_pw_d, w_br, w_out, ln_g, ln_b):
    bf16 = jnp.bfloat16
    row = lambda v: v[l].reshape(1, -1)
    blocks = [w_in[l, :, c * BR_W:(c + 1) * BR_W] * (0.5 if c in HALF_SCALED_COLS else 1.0)
              for c in HALO_COLS + TILE_COLS]
    w_in_l = jnp.concatenate(blocks + [0.5 * w_in[l, :, GATE_COL0:]], axis=1)
    b_s_full = jnp.repeat(b_s[l].T, GROUP_W, axis=1)
    w_pool_bd = jax.scipy.linalg.block_diag(*[w_pool[l, g] for g in range(N_GROUPS)])
    return (
        w_in_l.astype(bf16), conv_a[l], row(ln_v_g), row(ln_v_b),
        w_s[l].reshape(N_GROUPS * CHUNK, CHUNK).astype(bf16), b_s_full,
        w_pool_bd.astype(bf16), row(pool_scale), conv_d[l], row(conv_d_b),
        0.5 * row(ln_d_g), 0.5 * row(ln_d_b),
        w_pw_d[l].astype(bf16), (0.5 * w_br[l]).astype(bf16), w_out[l].astype(bf16), row(ln_g), row(ln_b),
    )


def kernel(x_prompt, x_sample, w_in, conv_a, ln_v_g, ln_v_b, w_s, b_s, w_pool, pool_scale, conv_d, conv_d_b, ln_d_g, ln_d_b, w_pw_d, w_br, w_out, ln_g, ln_b):
    params = (w_in, conv_a, ln_v_g, ln_v_b, w_s, b_s, w_pool, pool_scale, conv_d, conv_d_b,
              ln_d_g, ln_d_b, w_pw_d, w_br, w_out, ln_g, ln_b)
    y_prompt, y_sample = x_prompt, x_sample
    for l in range(DEPTH):
        weights = _prepare_layer_weights(l, *params)
        y_prompt = _encoder_layer(y_prompt, weights)
        y_sample = _encoder_layer(y_sample, weights)
    return (y_prompt, y_sample)
```

```python
import functools

import jax
import jax.numpy as jnp
from jax import lax
from jax.experimental import pallas as pl
from jax.experimental.pallas import tpu as pltpu

D_MODEL = 1024
DEPTH = 2
N_BRANCH = 4
BR_W = D_MODEL // N_BRANCH
N_GROUPS = 4
GROUP_W = BR_W // N_GROUPS
CHUNK = 128
POOL_WINDOWS = (2, 4, 8, 16)
CONV_A_W = 3
CONV_D_W = 31
ALPHA = (2 * DEPTH) ** 0.25
LN_EPS = 1e-5
N_BR_COLS = 12

LANES = 128
SUBLANES = 8
N_HALF = BR_W // LANES
TILE = 1024
HALO = 16
PHASES = 4
PHASE_ROWS = 64
OUT_ROWS = 256
VMEM_LIMIT_BYTES = 60 * 1024 * 1024

(COL_A_H, COL_A_B, COL_A_C, COL_A_Z, COL_B_U, COL_B_V, COL_B_Z, COL_C_P, COL_C_Z,
 COL_D_A, COL_D_G, COL_D_Z) = range(N_BR_COLS)
HALO_COLS = (COL_A_H, COL_A_C, COL_C_P, COL_D_A, COL_D_G, COL_A_B)
TILE_COLS = (COL_A_Z, COL_B_U, COL_B_V, COL_B_Z, COL_C_Z, COL_D_Z)
GATE_COL0 = N_BR_COLS * BR_W
HALF_SCALED_COLS = (COL_A_Z, COL_B_Z, COL_C_Z, COL_D_Z, COL_D_A, COL_D_G)
BR_A, BR_B, BR_C, BR_D = range(N_BRANCH)


def _gate(h):
    return jnp.tanh(h) + 1.0


def _half_silu(h):
    return h * _gate(h)


def _layer_norm(x, g, b):
    mu = jnp.mean(x, axis=-1, keepdims=True)
    xc = x - mu
    var = jnp.mean(xc * xc, axis=-1, keepdims=True)
    return xc * lax.rsqrt(var + LN_EPS) * g + b


def _dot(a, b):
    return jnp.dot(a, b, preferred_element_type=jnp.float32)


def _phase_blocks():
    return [tau + PHASES * m0 for tau in range(PHASES) for m0 in range(0, TILE // PHASES, PHASE_ROWS)]


def _phase_rows(ref, row0):
    return ref[pl.ds(row0, PHASE_ROWS, stride=PHASES), :]


def _store_halves(refs, v):
    for h, ref in enumerate(refs):
        ref[...] = v[:, h * LANES:(h + 1) * LANES]


def _load_halves(refs):
    return jnp.concatenate([ref[...] for ref in refs], axis=1)


def _layer_kernel(seq_len, x_ref, xp_ref, xn_ref, w_in_ref, conv_a_ref, ln_v_g_ref, ln_v_b_ref,
                  w_s_ref, b_s_ref, w_pool_ref, pool_scale_ref, conv_d_ref, conv_d_b_ref,
                  ln_d_g_ref, ln_d_b_ref, w_pw_d_ref, w_br_ref, w_out_ref, ln_g_ref, ln_b_ref,
                  o_ref,
                  hh0_ref, hh1_ref, ua0_ref, ua1_ref, p0_ref, p1_ref,
                  cd0_ref, cd1_ref, ca0_ref, ca1_ref, pl0_ref, pl1_ref):
    hh_refs, ua_refs, p_refs = (hh0_ref, hh1_ref), (ua0_ref, ua1_ref), (p0_ref, p1_ref)
    cd_refs, ca_refs, pool_refs = (cd0_ref, cd1_ref), (ca0_ref, ca1_ref), (pl0_ref, pl1_ref)
    t = pl.program_id(1)
    n_t = pl.num_programs(1)
    bf16 = jnp.bfloat16
    f32 = jnp.float32

    x = x_ref[0]
    xp = jnp.where(t > 0, xp_ref[0], 0.0)
    xn = jnp.where(t < n_t - 1, xn_ref[0], 0.0)
    x_bf = x.astype(bf16)
    xh_bf = jnp.concatenate([xp.astype(bf16), x_bf, xn.astype(bf16)], axis=0)

    def block(v, cols, col):
        i = cols.index(col)
        return v[:, i * BR_W:(i + 1) * BR_W]

    def gated_branch(i, y):
        gate = _gate(_dot(x_bf, w_in_ref[:, GATE_COL0 + i * D_MODEL:GATE_COL0 + (i + 1) * D_MODEL]))
        return gate * _dot(y.astype(bf16), w_br_ref[i])

    n_halo_cols = len(HALO_COLS) * BR_W
    ph = _dot(xh_bf, w_in_ref[:, 0:n_halo_cols])
    _store_halves(hh_refs, block(ph, HALO_COLS, COL_D_A) * _gate(block(ph, HALO_COLS, COL_D_G)))
    _store_halves(ua_refs, block(ph, HALO_COLS, COL_A_C) * block(ph, HALO_COLS, COL_A_H))
    _store_halves(p_refs, block(ph, HALO_COLS, COL_C_P))
    a_b = block(ph, HALO_COLS, COL_A_B)[HALO:HALO + TILE]
    pt = _dot(x_bf, w_in_ref[:, n_halo_cols:GATE_COL0])

    conv_d = conv_d_ref[...]
    conv_d_b = conv_d_b_ref[...]
    for h in range(N_HALF):
        w_h = conv_d[:, h * LANES:(h + 1) * LANES]
        for m0 in range(0, TILE, PHASES * PHASE_ROWS):
            acc = [None] * PHASES
            for off in range(CONV_D_W + PHASES - 1):
                v = _phase_rows(hh_refs[h], HALO - CONV_D_W // 2 + off + m0)
                for tau in range(PHASES):
                    k = off - tau
                    if 0 <= k < CONV_D_W:
                        term = v * w_h[k:k + 1]
                        acc[tau] = term if acc[tau] is None else acc[tau] + term
            for tau in range(PHASES):
                cd_refs[h][pl.ds(tau + m0, PHASE_ROWS, stride=PHASES), :] = acc[tau] + conv_d_b[:, h * LANES:(h + 1) * LANES]

    conv_a = conv_a_ref[...]
    for h in range(N_HALF):
        for row0 in _phase_blocks():
            acc = None
            for k in range(CONV_A_W):
                term = _phase_rows(ua_refs[h], HALO - CONV_A_W // 2 + k + row0) * conv_a[k:k + 1, h * LANES:(h + 1) * LANES]
                acc = term if acc is None else acc + term
            ca_refs[h][pl.ds(row0, PHASE_ROWS, stride=PHASES), :] = acc
    y_a = a_b * _load_halves(ca_refs) * _half_silu(block(pt, TILE_COLS, COL_A_Z))
    merged = gated_branch(BR_A, y_a)

    lo_group = lax.broadcasted_iota(jnp.int32, (PHASE_ROWS, LANES), 1) < GROUP_W
    row_step = PHASES * lax.broadcasted_iota(jnp.int32, (PHASE_ROWS, LANES), 0)
    for h in range(N_HALF):
        half_lo, half_hi = POOL_WINDOWS[2 * h] // 2, POOL_WINDOWS[2 * h + 1] // 2
        half = jnp.where(lo_group, half_lo, half_hi)
        for row0 in _phase_blocks():
            def window(d0, d1):
                s = None
                for d in range(d0, d1):
                    v = _phase_rows(p_refs[h], HALO + d + row0)
                    s = v if s is None else s + v
                return s
            inner = window(-half_lo, half_lo)
            outer = inner + window(-half_hi, -half_lo) + window(half_lo, half_hi)
            pos = t * TILE + row0 + row_step
            cnt = (jnp.minimum(pos + half, seq_len) - jnp.maximum(pos - half, 0)).astype(f32)
            pooled = jnp.where(lo_group, inner, outer) / cnt - _phase_rows(p_refs[h], HALO + row0)
            pool_refs[h][pl.ds(row0, PHASE_ROWS, stride=PHASES), :] = pooled
    pooled = _load_halves(pool_refs).astype(bf16)
    y_c = _dot(pooled, w_pool_ref[...]) * pool_scale_ref[...] * _half_silu(block(pt, TILE_COLS, COL_C_Z))
    merged = merged + gated_branch(BR_C, y_c)

    vn = _layer_norm(block(pt, TILE_COLS, COL_B_V), ln_v_g_ref[...], ln_v_b_ref[...]).astype(bf16)
    n_chunks = TILE // CHUNK
    vn_wide = jnp.concatenate([vn[c * CHUNK:(c + 1) * CHUNK] for c in range(n_chunks)], axis=1)
    r = _dot(w_s_ref[...], vn_wide)
    b_s = b_s_ref[...]
    lo_half = lax.broadcasted_iota(jnp.int32, (CHUNK, 2 * GROUP_W), 1) < GROUP_W
    mixed_chunks = []
    for c in range(n_chunks):
        c0 = c * BR_W
        left = jnp.where(lo_half, r[0:CHUNK, c0:c0 + 2 * GROUP_W], r[CHUNK:2 * CHUNK, c0:c0 + 2 * GROUP_W])
        right = jnp.where(lo_half, r[2 * CHUNK:3 * CHUNK, c0 + 2 * GROUP_W:c0 + BR_W], r[3 * CHUNK:, c0 + 2 * GROUP_W:c0 + BR_W])
        mixed_chunks.append(jnp.concatenate([left, right], axis=1) + b_s)
    mixed = jnp.concatenate(mixed_chunks, axis=0)
    y_b = block(pt, TILE_COLS, COL_B_U) * mixed * _half_silu(block(pt, TILE_COLS, COL_B_Z))
    merged = merged + gated_branch(BR_B, y_b)

    hd = _half_silu(_layer_norm(_load_halves(cd_refs), ln_d_g_ref[...], ln_d_b_ref[...])).astype(bf16)
    y_d = _dot(hd, w_pw_d_ref[...]) * _half_silu(block(pt, TILE_COLS, COL_D_Z))
    merged = (merged + gated_branch(BR_D, y_d)).astype(bf16)

    ln_g = ln_g_ref[...]
    ln_b = ln_b_ref[...]
    for r0 in range(0, TILE, OUT_ROWS):
        out = _dot(merged[r0:r0 + OUT_ROWS], w_out_ref[...])
        o_ref[0, r0:r0 + OUT_ROWS] = _layer_norm(ALPHA * x[r0:r0 + OUT_ROWS] + out, ln_g, ln_b)


def _resident(shape):
    return pl.BlockSpec(shape, lambda b, t: (0,) * len(shape), pipeline_mode=pl.Buffered(1))


def _encoder_layer(x, weights):
    bn, s, d = x.shape
    assert d == D_MODEL and s % TILE == 0 and TILE % CHUNK == 0 and TILE % HALO == 0
    assert HALO % SUBLANES == 0 and HALO >= CONV_D_W // 2 and HALO >= max(POOL_WINDOWS) // 2
    assert TILE % (PHASES * PHASE_ROWS) == 0 and TILE % OUT_ROWS == 0 and N_HALF * 2 == N_GROUPS
    n_t = s // TILE
    halo_per_tile = TILE // HALO
    n_halo_blocks = s // HALO
    in_specs = [
        pl.BlockSpec((1, TILE, d), lambda b, t: (b, t, 0)),
        pl.BlockSpec((1, HALO, d), lambda b, t: (b, jnp.maximum(t * halo_per_tile - 1, 0), 0)),
        pl.BlockSpec((1, HALO, d), lambda b, t: (b, jnp.minimum((t + 1) * halo_per_tile, n_halo_blocks - 1), 0)),
    ] + [_resident(w.shape) for w in weights]
    halo_scratch = pltpu.VMEM((TILE + 2 * HALO, LANES), jnp.float32)
    tile_scratch = pltpu.VMEM((TILE, LANES), jnp.float32)
    return pl.pallas_call(
        functools.partial(_layer_kernel, s),
        grid=(bn, n_t),
        in_specs=in_specs,
        out_specs=pl.BlockSpec((1, TILE, d), lambda b, t: (b, t, 0)),
        out_shape=jax.ShapeDtypeStruct(x.shape, x.dtype),
        scratch_shapes=[halo_scratch] * (3 * N_HALF) + [tile_scratch] * (3 * N_HALF),
        compiler_params=pltpu.CompilerParams(
            dimension_semantics=("parallel", "parallel"),
            vmem_limit_bytes=VMEM_LIMIT_BYTES),
        name="encoder_layer",
    )(x, x, x, *weights)


def _prepare_layer_weights(l, w_in, conv_a, ln_v_g, ln_v_b, w_s, b_s, w_pool, pool_scale,
                           conv_d, conv_d_b, ln_d_g, ln_d_b, w_pw_d, w_br, w_out, ln_g, ln_b):
    bf16 = jnp.bfloat16
    row = lambda v: v[l].reshape(1, -1)
    blocks = [w_in[l, :, c * BR_W:(c + 1) * BR_W] * (0.5 if c in HALF_SCALED_COLS else 1.0)
              for c in HALO_COLS + TILE_COLS]
    w_in_l = jnp.concatenate(blocks + [0.5 * w_in[l, :, GATE_COL0:]], axis=1)
    b_s_full = jnp.repeat(b_s[l].T, GROUP_W, axis=1)
    w_pool_bd = jax.scipy.linalg.block_diag(*[w_pool[l, g] for g in range(N_GROUPS)])
    return (
        w_in_l.astype(bf16), conv_a[l], row(ln_v_g), row(ln_v_b),
        w_s[l].reshape(N_GROUPS * CHUNK, CHUNK).astype(bf16), b_s_full,
        w_pool_bd.astype(bf16), row(pool_scale), conv_d[l], row(conv_d_b),
        0.5 * row(ln_d_g), 0.5 * row(ln_d_b),
        w_pw_d[l].astype(bf16), (0.5 * w_br[l]).astype(bf16), w_out[l].astype(bf16), row(ln_g), row(ln_b),
    )


def kernel(x_prompt, x_sample, w_in, conv_a, ln_v_g, ln_v_b, w_s, b_s, w_pool, pool_scale, conv_d, conv_d_b, ln_d_g, ln_d_b, w_pw_d, w_br, w_out, ln_g, ln_b):
    params = (w_in, conv_a, ln_v_g, ln_v_b, w_s, b_s, w_pool, pool_scale, conv_d, conv_d_b,
              ln_d_g, ln_d_b, w_pw_d, w_br, w_out, ln_g, ln_b)
    y_prompt, y_sample = x_prompt, x_sample
    for l in range(DEPTH):
        weights = _prepare_layer_weights(l, *params)
        y_prompt = _encoder_layer(y_prompt, weights)
        y_sample = _encoder_layer(y_sample, weights)
    return (y_prompt, y_sample)
```

```python
import functools

import jax
import jax.numpy as jnp
from jax import lax
from jax.experimental import pallas as pl
from jax.experimental.pallas import tpu as pltpu

D_MODEL = 1024
DEPTH = 2
N_BRANCH = 4
BR_W = D_MODEL // N_BRANCH
N_GROUPS = 4
GROUP_W = BR_W // N_GROUPS
CHUNK = 128
POOL_WINDOWS = (2, 4, 8, 16)
CONV_A_W = 3
CONV_D_W = 31
ALPHA = (2 * DEPTH) ** 0.25
LN_EPS = 1e-5
N_BR_COLS = 12

LANES = 128
SUBLANES = 8
N_HALF = BR_W // LANES
TILE = 512
HALO = 16
PHASES = 4
PHASE_ROWS = 64
OUT_ROWS = 256
VMEM_LIMIT_BYTES = 56 * 1024 * 1024

(COL_A_H, COL_A_B, COL_A_C, COL_A_Z, COL_B_U, COL_B_V, COL_B_Z, COL_C_P, COL_C_Z,
 COL_D_A, COL_D_G, COL_D_Z) = range(N_BR_COLS)
HALO_COLS = (COL_A_H, COL_A_C, COL_C_P, COL_D_A, COL_D_G, COL_A_B)
TILE_COLS = (COL_A_Z, COL_B_U, COL_B_V, COL_B_Z, COL_C_Z, COL_D_Z)
GATE_COL0 = N_BR_COLS * BR_W
HALF_SCALED_COLS = (COL_A_Z, COL_B_Z, COL_C_Z, COL_D_Z, COL_D_A, COL_D_G)
BR_A, BR_B, BR_C, BR_D = range(N_BRANCH)


def _gate(h):
    return jnp.tanh(h) + 1.0


def _half_silu(h):
    return h * _gate(h)


def _layer_norm(x, g, b):
    mu = jnp.mean(x, axis=-1, keepdims=True)
    xc = x - mu
    var = jnp.mean(xc * xc, axis=-1, keepdims=True)
    return xc * lax.rsqrt(var + LN_EPS) * g + b


def _dot(a, b):
    return jnp.dot(a, b, preferred_element_type=jnp.float32)


def _phase_blocks():
    return [tau + PHASES * m0 for tau in range(PHASES) for m0 in range(0, TILE // PHASES, PHASE_ROWS)]


def _phase_rows(ref, row0):
    return ref[pl.ds(row0, PHASE_ROWS, stride=PHASES), :]


def _store_halves(refs, v):
    for h, ref in enumerate(refs):
        ref[...] = v[:, h * LANES:(h + 1) * LANES]


def _load_halves(refs):
    return jnp.concatenate([ref[...] for ref in refs], axis=1)


def _layer_kernel(seq_len, x_ref, xp_ref, xn_ref, w_proj_ref, w_gate_ref, conv_a_ref, ln_v_g_ref, ln_v_b_ref,
                  w_s_ref, b_s_ref, w_pool_ref, pool_scale_ref, conv_d_ref, conv_d_b_ref,
                  ln_d_g_ref, ln_d_b_ref, w_pw_d_ref, w_br_ref, w_out_ref, ln_g_ref, ln_b_ref,
                  o_ref,
                  hh0_ref, hh1_ref, ua0_ref, ua1_ref, p0_ref, p1_ref,
                  cd0_ref, cd1_ref, ca0_ref, ca1_ref, pl0_ref, pl1_ref):
    hh_refs, ua_refs, p_refs = (hh0_ref, hh1_ref), (ua0_ref, ua1_ref), (p0_ref, p1_ref)
    cd_refs, ca_refs, pool_refs = (cd0_ref, cd1_ref), (ca0_ref, ca1_ref), (pl0_ref, pl1_ref)
    t = pl.program_id(1)
    n_t = pl.num_programs(1)
    bf16 = jnp.bfloat16
    f32 = jnp.float32

    x = x_ref[0]
    xp = jnp.where(t > 0, xp_ref[0], 0.0)
    xn = jnp.where(t < n_t - 1, xn_ref[0], 0.0)
    x_bf = x.astype(bf16)
    xh_bf = jnp.concatenate([xp.astype(bf16), x_bf, xn.astype(bf16)], axis=0)

    def block(v, cols, col):
        i = cols.index(col)
        return v[:, i * BR_W:(i + 1) * BR_W]

    def gated_branch(i, y):
        gate = _gate(_dot(x_bf, w_gate_ref[:, i * D_MODEL:(i + 1) * D_MODEL]))
        return gate * _dot(y.astype(bf16), w_br_ref[i])

    n_halo_cols = len(HALO_COLS) * BR_W
    ph = _dot(xh_bf, w_proj_ref[:, 0:n_halo_cols])
    _store_halves(hh_refs, block(ph, HALO_COLS, COL_D_A) * _gate(block(ph, HALO_COLS, COL_D_G)))
    _store_halves(ua_refs, block(ph, HALO_COLS, COL_A_C) * block(ph, HALO_COLS, COL_A_H))
    _store_halves(p_refs, block(ph, HALO_COLS, COL_C_P))
    a_b = block(ph, HALO_COLS, COL_A_B)[HALO:HALO + TILE]
    pt = _dot(x_bf, w_proj_ref[:, n_halo_cols:])

    conv_d = conv_d_ref[...]
    conv_d_b = conv_d_b_ref[...]
    for h in range(N_HALF):
        w_h = conv_d[:, h * LANES:(h + 1) * LANES]
        for m0 in range(0, TILE, PHASES * PHASE_ROWS):
            acc = [None] * PHASES
            for off in range(CONV_D_W + PHASES - 1):
                v = _phase_rows(hh_refs[h], HALO - CONV_D_W // 2 + off + m0)
                for tau in range(PHASES):
                    k = off - tau
                    if 0 <= k < CONV_D_W:
                        term = v * w_h[k:k + 1]
                        acc[tau] = term if acc[tau] is None else acc[tau] + term
            for tau in range(PHASES):
                cd_refs[h][pl.ds(tau + m0, PHASE_ROWS, stride=PHASES), :] = acc[tau] + conv_d_b[:, h * LANES:(h + 1) * LANES]

    conv_a = conv_a_ref[...]
    for h in range(N_HALF):
        for row0 in _phase_blocks():
            acc = None
            for k in range(CONV_A_W):
                term = _phase_rows(ua_refs[h], HALO - CONV_A_W // 2 + k + row0) * conv_a[k:k + 1, h * LANES:(h + 1) * LANES]
                acc = term if acc is None else acc + term
            ca_refs[h][pl.ds(row0, PHASE_ROWS, stride=PHASES), :] = acc
    y_a = a_b * _load_halves(ca_refs) * _half_silu(block(pt, TILE_COLS, COL_A_Z))
    merged = gated_branch(BR_A, y_a)

    lo_group = lax.broadcasted_iota(jnp.int32, (PHASE_ROWS, LANES), 1) < GROUP_W
    row_step = PHASES * lax.broadcasted_iota(jnp.int32, (PHASE_ROWS, LANES), 0)
    for h in range(N_HALF):
        half_lo, half_hi = POOL_WINDOWS[2 * h] // 2, POOL_WINDOWS[2 * h + 1] // 2
        half = jnp.where(lo_group, half_lo, half_hi)
        for row0 in _phase_blocks():
            def window(d0, d1):
                s = None
                for d in range(d0, d1):
                    v = _phase_rows(p_refs[h], HALO + d + row0)
                    s = v if s is None else s + v
                return s
            inner = window(-half_lo, half_lo)
            outer = inner + window(-half_hi, -half_lo) + window(half_lo, half_hi)
            pos = t * TILE + row0 + row_step
            cnt = (jnp.minimum(pos + half, seq_len) - jnp.maximum(pos - half, 0)).astype(f32)
            pooled = jnp.where(lo_group, inner, outer) / cnt - _phase_rows(p_refs[h], HALO + row0)
            pool_refs[h][pl.ds(row0, PHASE_ROWS, stride=PHASES), :] = pooled
    pooled = _load_halves(pool_refs).astype(bf16)
    y_c = _dot(pooled, w_pool_ref[...]) * pool_scale_ref[...] * _half_silu(block(pt, TILE_COLS, COL_C_Z))
    merged = merged + gated_branch(BR_C, y_c)

    vn = _layer_norm(block(pt, TILE_COLS, COL_B_V), ln_v_g_ref[...], ln_v_b_ref[...]).astype(bf16)
    n_chunks = TILE // CHUNK
    lo_group_c = lax.broadcasted_iota(jnp.int32, (CHUNK, LANES), 1) < GROUP_W
    mixed_halves = []
    for h in range(N_HALF):
        vn_wide = jnp.concatenate([vn[c * CHUNK:(c + 1) * CHUNK, h * LANES:(h + 1) * LANES]
                                   for c in range(n_chunks)], axis=1)
        r = _dot(w_s_ref[2 * h * CHUNK:2 * (h + 1) * CHUNK, :], vn_wide)
        mixed_halves.append(jnp.concatenate(
            [jnp.where(lo_group_c, r[0:CHUNK, c * LANES:(c + 1) * LANES], r[CHUNK:2 * CHUNK, c * LANES:(c + 1) * LANES])
             for c in range(n_chunks)], axis=0))
    b_s = jnp.concatenate([b_s_ref[...]] * n_chunks, axis=0)
    mixed = jnp.concatenate(mixed_halves, axis=1) + b_s
    y_b = block(pt, TILE_COLS, COL_B_U) * mixed * _half_silu(block(pt, TILE_COLS, COL_B_Z))
    merged = merged + gated_branch(BR_B, y_b)

    hd = _half_silu(_layer_norm(_load_halves(cd_refs), ln_d_g_ref[...], ln_d_b_ref[...])).astype(bf16)
    y_d = _dot(hd, w_pw_d_ref[...]) * _half_silu(block(pt, TILE_COLS, COL_D_Z))
    merged = (merged + gated_branch(BR_D, y_d)).astype(bf16)

    ln_g = ln_g_ref[...]
    ln_b = ln_b_ref[...]
    for r0 in range(0, TILE, OUT_ROWS):
        out = _dot(merged[r0:r0 + OUT_ROWS], w_out_ref[...])
        o_ref[0, r0:r0 + OUT_ROWS] = _layer_norm(ALPHA * x[r0:r0 + OUT_ROWS] + out, ln_g, ln_b)


def _resident(shape):
    return pl.BlockSpec(shape, lambda b, t: (0,) * len(shape), pipeline_mode=pl.Buffered(1))


def _encoder_layer(x, weights):
    bn, s, d = x.shape
    assert d == D_MODEL and s % TILE == 0 and TILE % CHUNK == 0 and TILE % HALO == 0
    assert HALO % SUBLANES == 0 and HALO >= CONV_D_W // 2 and HALO >= max(POOL_WINDOWS) // 2
    assert TILE % (PHASES * PHASE_ROWS) == 0 and TILE % OUT_ROWS == 0 and N_HALF * 2 == N_GROUPS
    n_t = s // TILE
    halo_per_tile = TILE // HALO
    n_halo_blocks = s // HALO
    in_specs = [
        pl.BlockSpec((1, TILE, d), lambda b, t: (b, t, 0)),
        pl.BlockSpec((1, HALO, d), lambda b, t: (b, jnp.maximum(t * halo_per_tile - 1, 0), 0)),
        pl.BlockSpec((1, HALO, d), lambda b, t: (b, jnp.minimum((t + 1) * halo_per_tile, n_halo_blocks - 1), 0)),
    ] + [_resident(w.shape) for w in weights]
    halo_scratch = pltpu.VMEM((TILE + 2 * HALO, LANES), jnp.float32)
    tile_scratch = pltpu.VMEM((TILE, LANES), jnp.float32)
    return pl.pallas_call(
        functools.partial(_layer_kernel, s),
        grid=(bn, n_t),
        in_specs=in_specs,
        out_specs=pl.BlockSpec((1, TILE, d), lambda b, t: (b, t, 0)),
        out_shape=jax.ShapeDtypeStruct(x.shape, x.dtype),
        scratch_shapes=[halo_scratch] * (3 * N_HALF) + [tile_scratch] * (3 * N_HALF),
        compiler_params=pltpu.CompilerParams(
            dimension_semantics=("parallel", "parallel"),
            vmem_limit_bytes=VMEM_LIMIT_BYTES),
        name="encoder_layer",
    )(x, x, x, *weights)


def _prepare_weights(w_in, conv_a, ln_v_g, ln_v_b, w_s, b_s, w_pool, pool_scale,
                     conv_d, conv_d_b, ln_d_g, ln_d_b, w_pw_d, w_br, w_out, ln_g, ln_b):
    bf16 = jnp.bfloat16
    rows = lambda v: v.reshape(DEPTH, 1, -1)
    w_blocks = w_in.reshape(DEPTH, D_MODEL, -1, BR_W)
    w_proj = jnp.concatenate([w_blocks[:, :, c] * (0.5 if c in HALF_SCALED_COLS else 1.0)
                              for c in HALO_COLS + TILE_COLS], axis=-1).astype(bf16)
    w_gate = (0.5 * w_in[:, :, GATE_COL0:]).astype(bf16)
    b_s_full = jnp.repeat(jnp.swapaxes(b_s, 1, 2), GROUP_W, axis=2)
    eye = jnp.eye(N_GROUPS, dtype=w_pool.dtype)
    w_pool_bd = jnp.einsum("lgij,gh->lgihj", w_pool, eye).reshape(DEPTH, BR_W, BR_W)
    stacked = (
        w_proj, w_gate, conv_a, rows(ln_v_g), rows(ln_v_b),
        w_s.reshape(DEPTH, N_GROUPS * CHUNK, CHUNK).astype(bf16), b_s_full,
        w_pool_bd.astype(bf16), rows(pool_scale), conv_d, rows(conv_d_b),
        0.5 * rows(ln_d_g), 0.5 * rows(ln_d_b),
        w_pw_d.astype(bf16), (0.5 * w_br).astype(bf16), w_out.astype(bf16), rows(ln_g), rows(ln_b),
    )
    return [tuple(w[l] for w in stacked) for l in range(DEPTH)]


def kernel(x_prompt, x_sample, w_in, conv_a, ln_v_g, ln_v_b, w_s, b_s, w_pool, pool_scale, conv_d, conv_d_b, ln_d_g, ln_d_b, w_pw_d, w_br, w_out, ln_g, ln_b):
    layer_weights = _prepare_weights(w_in, conv_a, ln_v_g, ln_v_b, w_s, b_s, w_pool, pool_scale, conv_d, conv_d_b,
                                     ln_d_g, ln_d_b, w_pw_d, w_br, w_out, ln_g, ln_b)
    y_prompt, y_sample = x_prompt, x_sample
    for weights in layer_weights:
        y_prompt = _encoder_layer(y_prompt, weights)
        y_sample = _encoder_layer(y_sample, weights)
    return (y_prompt, y_sample)
```

```python
import functools

import jax
import jax.numpy as jnp
from jax import lax
from jax.experimental import pallas as pl
from jax.experimental.pallas import tpu as pltpu

D_MODEL = 1024
DEPTH = 2
N_BRANCH = 4
BR_W = D_MODEL // N_BRANCH
N_GROUPS = 4
GROUP_W = BR_W // N_GROUPS
CHUNK = 128
POOL_WINDOWS = (2, 4, 8, 16)
CONV_A_W = 3
CONV_D_W = 31
ALPHA = (2 * DEPTH) ** 0.25
LN_EPS = 1e-5
N_BR_COLS = 12

LANES = 128
SUBLANES = 8
N_HALF = BR_W // LANES
TILE = 512
HALO = 16
PHASES = 4
PHASE_ROWS = 64
OUT_ROWS = 256
VMEM_LIMIT_BYTES = 56 * 1024 * 1024

(COL_A_H, COL_A_B, COL_A_C, COL_A_Z, COL_B_U, COL_B_V, COL_B_Z, COL_C_P, COL_C_Z,
 COL_D_A, COL_D_G, COL_D_Z) = range(N_BR_COLS)
HALO_COLS = (COL_A_H, COL_A_C, COL_C_P, COL_D_A, COL_D_G, COL_A_B)
TILE_COLS = (COL_A_Z, COL_B_U, COL_B_V, COL_B_Z, COL_C_Z, COL_D_Z)
GATE_COL0 = N_BR_COLS * BR_W
HALF_SCALED_COLS = (COL_A_Z, COL_B_Z, COL_C_Z, COL_D_Z, COL_D_A, COL_D_G)
BR_A, BR_B, BR_C, BR_D = range(N_BRANCH)


def _gate(h):
    return jnp.tanh(h) + 1.0


def _half_silu(h):
    return h * _gate(h)


def _layer_norm(x, g, b):
    mu = jnp.mean(x, axis=-1, keepdims=True)
    xc = x - mu
    var = jnp.mean(xc * xc, axis=-1, keepdims=True)
    return xc * lax.rsqrt(var + LN_EPS) * g + b


def _dot(a, b):
    return jnp.dot(a, b, preferred_element_type=jnp.float32)


def _phase_blocks():
    return [tau + PHASES * m0 for tau in range(PHASES) for m0 in range(0, TILE // PHASES, PHASE_ROWS)]


def _phase_rows(ref, row0):
    return ref[pl.ds(row0, PHASE_ROWS, stride=PHASES), :]


def _store_halves(refs, v):
    for h, ref in enumerate(refs):
        ref[...] = v[:, h * LANES:(h + 1) * LANES]


def _load_halves(refs):
    return jnp.concatenate([ref[...] for ref in refs], axis=1)


def _layer_kernel(seq_len, x_ref, xp_ref, xn_ref, w_in_ref, conv_a_ref, ln_v_g_ref, ln_v_b_ref,
                  w_s_ref, b_s_ref, w_pool_ref, pool_scale_ref, conv_d_ref, conv_d_b_ref,
                  ln_d_g_ref, ln_d_b_ref, w_pw_d_ref, w_br_ref, w_out_ref, ln_g_ref, ln_b_ref,
                  o_ref,
                  hh0_ref, hh1_ref, ua0_ref, ua1_ref, p0_ref, p1_ref,
                  cd0_ref, cd1_ref, ca0_ref, ca1_ref, pl0_ref, pl1_ref):
    hh_refs, ua_refs, p_refs = (hh0_ref, hh1_ref), (ua0_ref, ua1_ref), (p0_ref, p1_ref)
    cd_refs, ca_refs, pool_refs = (cd0_ref, cd1_ref), (ca0_ref, ca1_ref), (pl0_ref, pl1_ref)
    t = pl.program_id(1)
    n_t = pl.num_programs(1)
    bf16 = jnp.bfloat16
    f32 = jnp.float32

    x = x_ref[0]
    xp = jnp.where(t > 0, xp_ref[0], 0.0)
    xn = jnp.where(t < n_t - 1, xn_ref[0], 0.0)
    x_bf = x.astype(bf16)
    xh_bf = jnp.concatenate([xp.astype(bf16), x_bf, xn.astype(bf16)], axis=0)

    def block(v, cols, col):
        i = cols.index(col)
        return v[:, i * BR_W:(i + 1) * BR_W]

    def gated_branch(i, y):
        gate = _gate(_dot(x_bf, w_in_ref[:, GATE_COL0 + i * D_MODEL:GATE_COL0 + (i + 1) * D_MODEL]))
        return gate * _dot(y.astype(bf16), w_br_ref[i])

    n_halo_cols = len(HALO_COLS) * BR_W
    ph = _dot(xh_bf, w_in_ref[:, 0:n_halo_cols])
    _store_halves(hh_refs, block(ph, HALO_COLS, COL_D_A) * _gate(block(ph, HALO_COLS, COL_D_G)))
    _store_halves(ua_refs, block(ph, HALO_COLS, COL_A_C) * block(ph, HALO_COLS, COL_A_H))
    _store_halves(p_refs, block(ph, HALO_COLS, COL_C_P))
    a_b = block(ph, HALO_COLS, COL_A_B)[HALO:HALO + TILE]
    pt = _dot(x_bf, w_in_ref[:, n_halo_cols:GATE_COL0])

    conv_d = conv_d_ref[...]
    conv_d_b = conv_d_b_ref[...]
    for h in range(N_HALF):
        w_h = conv_d[:, h * LANES:(h + 1) * LANES]
        for m0 in range(0, TILE, PHASES * PHASE_ROWS):
            acc = [None] * PHASES
            for off in range(CONV_D_W + PHASES - 1):
                v = _phase_rows(hh_refs[h], HALO - CONV_D_W // 2 + off + m0)
                for tau in range(PHASES):
                    k = off - tau
                    if 0 <= k < CONV_D_W:
                        term = v * w_h[k:k + 1]
                        acc[tau] = term if acc[tau] is None else acc[tau] + term
            for tau in range(PHASES):
                cd_refs[h][pl.ds(tau + m0, PHASE_ROWS, stride=PHASES), :] = acc[tau] + conv_d_b[:, h * LANES:(h + 1) * LANES]

    conv_a = conv_a_ref[...]
    for h in range(N_HALF):
        for row0 in _phase_blocks():
            acc = None
            for k in range(CONV_A_W):
                term = _phase_rows(ua_refs[h], HALO - CONV_A_W // 2 + k + row0) * conv_a[k:k + 1, h * LANES:(h + 1) * LANES]
                acc = term if acc is None else acc + term
            ca_refs[h][pl.ds(row0, PHASE_ROWS, stride=PHASES), :] = acc
    y_a = a_b * _load_halves(ca_refs) * _half_silu(block(pt, TILE_COLS, COL_A_Z))
    merged = gated_branch(BR_A, y_a)

    lo_group = lax.broadcasted_iota(jnp.int32, (PHASE_ROWS, LANES), 1) < GROUP_W
    row_step = PHASES * lax.broadcasted_iota(jnp.int32, (PHASE_ROWS, LANES), 0)
    for h in range(N_HALF):
        half_lo, half_hi = POOL_WINDOWS[2 * h] // 2, POOL_WINDOWS[2 * h + 1] // 2
        half = jnp.where(lo_group, half_lo, half_hi)
        for row0 in _phase_blocks():
            def window(d0, d1):
                s = None
                for d in range(d0, d1):
                    v = _phase_rows(p_refs[h], HALO + d + row0)
                    s = v if s is None else s + v
                return s
            inner = window(-half_lo, half_lo)
            outer = inner + window(-half_hi, -half_lo) + window(half_lo, half_hi)
            pos = t * TILE + row0 + row_step
            cnt = (jnp.minimum(pos + half, seq_len) - jnp.maximum(pos - half, 0)).astype(f32)
            pooled = jnp.where(lo_group, inner, outer) / cnt - _phase_rows(p_refs[h], HALO + row0)
            pool_refs[h][pl.ds(row0, PHASE_ROWS, stride=PHASES), :] = pooled
    pooled = _load_halves(pool_refs).astype(bf16)
    y_c = _dot(pooled, w_pool_ref[...]) * pool_scale_ref[...] * _half_silu(block(pt, TILE_COLS, COL_C_Z))
    merged = merged + gated_branch(BR_C, y_c)

    vn = _layer_norm(block(pt, TILE_COLS, COL_B_V), ln_v_g_ref[...], ln_v_b_ref[...]).astype(bf16)
    n_chunks = TILE // CHUNK
    lo_group_c = lax.broadcasted_iota(jnp.int32, (CHUNK, LANES), 1) < GROUP_W
    mixed_halves = []
    for h in range(N_HALF):
        vn_wide = jnp.concatenate([vn[c * CHUNK:(c + 1) * CHUNK, h * LANES:(h + 1) * LANES]
                                   for c in range(n_chunks)], axis=1)
        r = _dot(w_s_ref[2 * h * CHUNK:2 * (h + 1) * CHUNK, :], vn_wide)
        mixed_halves.append(jnp.concatenate(
            [jnp.where(lo_group_c, r[0:CHUNK, c * LANES:(c + 1) * LANES], r[CHUNK:2 * CHUNK, c * LANES:(c + 1) * LANES])
             for c in range(n_chunks)], axis=0))
    b_s = jnp.concatenate([b_s_ref[...]] * n_chunks, axis=0)
    mixed = jnp.concatenate(mixed_halves, axis=1) + b_s
    y_b = block(pt, TILE_COLS, COL_B_U) * mixed * _half_silu(block(pt, TILE_COLS, COL_B_Z))
    merged = merged + gated_branch(BR_B, y_b)

    hd = _half_silu(_layer_norm(_load_halves(cd_refs), ln_d_g_ref[...], ln_d_b_ref[...])).astype(bf16)
    y_d = _dot(hd, w_pw_d_ref[...]) * _half_silu(block(pt, TILE_COLS, COL_D_Z))
    merged = (merged + gated_branch(BR_D, y_d)).astype(bf16)

    ln_g = ln_g_ref[...]
    ln_b = ln_b_ref[...]
    for r0 in range(0, TILE, OUT_ROWS):
        out = _dot(merged[r0:r0 + OUT_ROWS], w_out_ref[...])
        o_ref[0, r0:r0 + OUT_ROWS] = _layer_norm(ALPHA * x[r0:r0 + OUT_ROWS] + out, ln_g, ln_b)


def _resident(stacked_shape, layer):
    index = (layer,) + (0,) * (len(stacked_shape) - 1)
    return pl.BlockSpec((None,) + tuple(stacked_shape[1:]), lambda b, t: index, pipeline_mode=pl.Buffered(1))


def _encoder_layer(x, weights, layer):
    bn, s, d = x.shape
    assert d == D_MODEL and s % TILE == 0 and TILE % CHUNK == 0 and TILE % HALO == 0
    assert HALO % SUBLANES == 0 and HALO >= CONV_D_W // 2 and HALO >= max(POOL_WINDOWS) // 2
    assert TILE % (PHASES * PHASE_ROWS) == 0 and TILE % OUT_ROWS == 0 and N_HALF * 2 == N_GROUPS
    n_t = s // TILE
    halo_per_tile = TILE // HALO
    n_halo_blocks = s // HALO
    in_specs = [
        pl.BlockSpec((1, TILE, d), lambda b, t: (b, t, 0)),
        pl.BlockSpec((1, HALO, d), lambda b, t: (b, jnp.maximum(t * halo_per_tile - 1, 0), 0)),
        pl.BlockSpec((1, HALO, d), lambda b, t: (b, jnp.minimum((t + 1) * halo_per_tile, n_halo_blocks - 1), 0)),
    ] + [_resident(w.shape, layer) for w in weights]
    halo_scratch = pltpu.VMEM((TILE + 2 * HALO, LANES), jnp.float32)
    tile_scratch = pltpu.VMEM((TILE, LANES), jnp.float32)
    return pl.pallas_call(
        functools.partial(_layer_kernel, s),
        grid=(bn, n_t),
        in_specs=in_specs,
        out_specs=pl.BlockSpec((1, TILE, d), lambda b, t: (b, t, 0)),
        out_shape=jax.ShapeDtypeStruct(x.shape, x.dtype),
        scratch_shapes=[halo_scratch] * (3 * N_HALF) + [tile_scratch] * (3 * N_HALF),
        compiler_params=pltpu.CompilerParams(
            dimension_semantics=("parallel", "parallel"),
            vmem_limit_bytes=VMEM_LIMIT_BYTES),
        name="encoder_layer",
    )(x, x, x, *weights)


def _w_in_prep_kernel(src_block_ref, halve_ref, w_ref, o_ref):
    j = pl.program_id(1)
    scale = jnp.where(halve_ref[j] == 1, 0.5, 1.0)
    o_ref[...] = (w_ref[...] * scale).astype(o_ref.dtype)


def _prepare_w_in(w_in):
    depth, d, in_cols = w_in.shape
    n_blocks = in_cols // BR_W
    src_block = HALO_COLS + TILE_COLS + tuple(range(N_BR_COLS, n_blocks))
    halve = tuple(int(c in HALF_SCALED_COLS or c >= N_BR_COLS) for c in src_block)
    grid_spec = pltpu.PrefetchScalarGridSpec(
        num_scalar_prefetch=2,
        grid=(depth, n_blocks),
        in_specs=[pl.BlockSpec((1, d, BR_W), lambda l, j, src, halve: (l, 0, src[j]))],
        out_specs=pl.BlockSpec((1, d, BR_W), lambda l, j, src, halve: (l, 0, j)),
    )
    return pl.pallas_call(
        _w_in_prep_kernel,
        grid_spec=grid_spec,
        out_shape=jax.ShapeDtypeStruct(w_in.shape, jnp.bfloat16),
        compiler_params=pltpu.CompilerParams(dimension_semantics=("parallel", "parallel")),
        name="w_in_prep",
    )(jnp.array(src_block, jnp.int32), jnp.array(halve, jnp.int32), w_in)


def _prepare_weights(w_in, conv_a, ln_v_g, ln_v_b, w_s, b_s, w_pool, pool_scale,
                     conv_d, conv_d_b, ln_d_g, ln_d_b, w_pw_d, w_br, w_out, ln_g, ln_b):
    bf16 = jnp.bfloat16
    rows = lambda v: v.reshape(DEPTH, 1, -1)
    b_s_full = jnp.repeat(jnp.swapaxes(b_s, 1, 2), GROUP_W, axis=2)
    eye = jnp.eye(N_GROUPS, dtype=w_pool.dtype)
    w_pool_bd = jnp.einsum("lgij,gh->lgihj", w_pool, eye).reshape(DEPTH, BR_W, BR_W)
    return (
        _prepare_w_in(w_in), conv_a, rows(ln_v_g), rows(ln_v_b),
        w_s.reshape(DEPTH, N_GROUPS * CHUNK, CHUNK).astype(bf16), b_s_full,
        w_pool_bd.astype(bf16), rows(pool_scale), conv_d, rows(conv_d_b),
        0.5 * rows(ln_d_g), 0.5 * rows(ln_d_b),
        w_pw_d.astype(bf16), (0.5 * w_br).astype(bf16), w_out.astype(bf16), rows(ln_g), rows(ln_b),
    )


def kernel(x_prompt, x_sample, w_in, conv_a, ln_v_g, ln_v_b, w_s, b_s, w_pool, pool_scale, conv_d, conv_d_b, ln_d_g, ln_d_b, w_pw_d, w_br, w_out, ln_g, ln_b):
    weights = _prepare_weights(w_in, conv_a, ln_v_g, ln_v_b, w_s, b_s, w_pool, pool_scale, conv_d, conv_d_b,
                               ln_d_g, ln_d_b, w_pw_d, w_br, w_out, ln_g, ln_b)
    y_prompt, y_sample = x_prompt, x_sample
    for layer in range(DEPTH):
        y_prompt = _encoder_layer(y_prompt, weights, layer)
        y_sample = _encoder_layer(y_sample, weights, layer)
    return (y_prompt, y_sample)
```

```python
import functools

import jax
import jax.numpy as jnp
from jax import lax
from jax.experimental import pallas as pl
from jax.experimental.pallas import tpu as pltpu

D_MODEL = 1024
DEPTH = 2
N_BRANCH = 4
BR_W = D_MODEL // N_BRANCH
N_GROUPS = 4
GROUP_W = BR_W // N_GROUPS
CHUNK = 128
POOL_WINDOWS = (2, 4, 8, 16)
CONV_A_W = 3
CONV_D_W = 31
ALPHA = (2 * DEPTH) ** 0.25
LN_EPS = 1e-5
N_BR_COLS = 12

LANES = 128
SUBLANES = 8
N_HALF = BR_W // LANES
TILE_CHOICES = (1024, 512)
MIN_GRID_STEPS = 64
HALO = 16
PHASES = 4
PHASE_ROWS = 64
OUT_ROWS = 256
VMEM_LIMIT_BYTES = 60 * 1024 * 1024

(COL_A_H, COL_A_B, COL_A_C, COL_A_Z, COL_B_U, COL_B_V, COL_B_Z, COL_C_P, COL_C_Z,
 COL_D_A, COL_D_G, COL_D_Z) = range(N_BR_COLS)
HALO_COLS = (COL_A_H, COL_A_C, COL_C_P, COL_D_A, COL_D_G, COL_A_B)
TILE_COLS = (COL_A_Z, COL_B_U, COL_B_V, COL_B_Z, COL_C_Z, COL_D_Z)
GATE_COL0 = N_BR_COLS * BR_W
HALF_SCALED_COLS = (COL_A_Z, COL_B_Z, COL_C_Z, COL_D_Z, COL_D_A, COL_D_G)
BR_A, BR_B, BR_C, BR_D = range(N_BRANCH)


def _gate(h):
    return jnp.tanh(h) + 1.0


def _half_silu(h):
    return h * _gate(h)


def _layer_norm(x, g, b):
    mu = jnp.mean(x, axis=-1, keepdims=True)
    xc = x - mu
    var = jnp.mean(xc * xc, axis=-1, keepdims=True)
    return xc * lax.rsqrt(var + LN_EPS) * g + b


def _dot(a, b):
    return jnp.dot(a, b, preferred_element_type=jnp.float32)


def _phase_blocks(tile):
    return [tau + PHASES * m0 for tau in range(PHASES) for m0 in range(0, tile // PHASES, PHASE_ROWS)]


def _phase_rows(ref, row0):
    return ref[pl.ds(row0, PHASE_ROWS, stride=PHASES), :]


def _store_halves(refs, v):
    for h, ref in enumerate(refs):
        ref[...] = v[:, h * LANES:(h + 1) * LANES]


def _load_halves(refs):
    return jnp.concatenate([ref[...] for ref in refs], axis=1)


def _layer_kernel(seq_len, x_ref, xp_ref, xn_ref, w_in_ref, conv_a_ref, ln_v_g_ref, ln_v_b_ref,
                  w_s_ref, b_s_ref, w_pool_ref, pool_scale_ref, conv_d_ref, conv_d_b_ref,
                  ln_d_g_ref, ln_d_b_ref, w_pw_d_ref, w_br_ref, w_out_ref, ln_g_ref, ln_b_ref,
                  o_ref,
                  hh0_ref, hh1_ref, ua0_ref, ua1_ref, p0_ref, p1_ref,
                  cd0_ref, cd1_ref, ca0_ref, ca1_ref, pl0_ref, pl1_ref):
    hh_refs, ua_refs, p_refs = (hh0_ref, hh1_ref), (ua0_ref, ua1_ref), (p0_ref, p1_ref)
    cd_refs, ca_refs, pool_refs = (cd0_ref, cd1_ref), (ca0_ref, ca1_ref), (pl0_ref, pl1_ref)
    t = pl.program_id(1)
    n_t = pl.num_programs(1)
    tile = x_ref.shape[1]
    bf16 = jnp.bfloat16
    f32 = jnp.float32

    x = x_ref[0]
    xp = jnp.where(t > 0, xp_ref[0], 0.0)
    xn = jnp.where(t < n_t - 1, xn_ref[0], 0.0)
    x_bf = x.astype(bf16)
    xh_bf = jnp.concatenate([xp.astype(bf16), x_bf, xn.astype(bf16)], axis=0)

    def block(v, cols, col):
        i = cols.index(col)
        return v[:, i * BR_W:(i + 1) * BR_W]

    def gated_branch(i, y):
        gate = _gate(_dot(x_bf, w_in_ref[:, GATE_COL0 + i * D_MODEL:GATE_COL0 + (i + 1) * D_MODEL]))
        return gate * _dot(y.astype(bf16), w_br_ref[i])

    n_halo_cols = len(HALO_COLS) * BR_W
    ph = _dot(xh_bf, w_in_ref[:, 0:n_halo_cols])
    _store_halves(hh_refs, block(ph, HALO_COLS, COL_D_A) * _gate(block(ph, HALO_COLS, COL_D_G)))
    _store_halves(ua_refs, block(ph, HALO_COLS, COL_A_C) * block(ph, HALO_COLS, COL_A_H))
    _store_halves(p_refs, block(ph, HALO_COLS, COL_C_P))
    a_b = block(ph, HALO_COLS, COL_A_B)[HALO:HALO + tile]
    pt = _dot(x_bf, w_in_ref[:, n_halo_cols:GATE_COL0])

    conv_d = conv_d_ref[...]
    conv_d_b = conv_d_b_ref[...]
    for h in range(N_HALF):
        w_h = conv_d[:, h * LANES:(h + 1) * LANES]
        for m0 in range(0, tile, PHASES * PHASE_ROWS):
            acc = [None] * PHASES
            for off in range(CONV_D_W + PHASES - 1):
                v = _phase_rows(hh_refs[h], HALO - CONV_D_W // 2 + off + m0)
                for tau in range(PHASES):
                    k = off - tau
                    if 0 <= k < CONV_D_W:
                        term = v * w_h[k:k + 1]
                        acc[tau] = term if acc[tau] is None else acc[tau] + term
            for tau in range(PHASES):
                cd_refs[h][pl.ds(tau + m0, PHASE_ROWS, stride=PHASES), :] = acc[tau] + conv_d_b[:, h * LANES:(h + 1) * LANES]

    conv_a = conv_a_ref[...]
    for h in range(N_HALF):
        for row0 in _phase_blocks(tile):
            acc = None
            for k in range(CONV_A_W):
                term = _phase_rows(ua_refs[h], HALO - CONV_A_W // 2 + k + row0) * conv_a[k:k + 1, h * LANES:(h + 1) * LANES]
                acc = term if acc is None else acc + term
            ca_refs[h][pl.ds(row0, PHASE_ROWS, stride=PHASES), :] = acc
    y_a = a_b * _load_halves(ca_refs) * _half_silu(block(pt, TILE_COLS, COL_A_Z))
    merged = gated_branch(BR_A, y_a)

    lo_group = lax.broadcasted_iota(jnp.int32, (PHASE_ROWS, LANES), 1) < GROUP_W
    row_step = PHASES * lax.broadcasted_iota(jnp.int32, (PHASE_ROWS, LANES), 0)
    for h in range(N_HALF):
        half_lo, half_hi = POOL_WINDOWS[2 * h] // 2, POOL_WINDOWS[2 * h + 1] // 2
        half = jnp.where(lo_group, half_lo, half_hi)
        for row0 in _phase_blocks(tile):
            def window(d0, d1):
                s = None
                for d in range(d0, d1):
                    v = _phase_rows(p_refs[h], HALO + d + row0)
                    s = v if s is None else s + v
                return s
            inner = window(-half_lo, half_lo)
            outer = inner + window(-half_hi, -half_lo) + window(half_lo, half_hi)
            pos = t * tile + row0 + row_step
            cnt = (jnp.minimum(pos + half, seq_len) - jnp.maximum(pos - half, 0)).astype(f32)
            pooled = jnp.where(lo_group, inner, outer) / cnt - _phase_rows(p_refs[h], HALO + row0)
            pool_refs[h][pl.ds(row0, PHASE_ROWS, stride=PHASES), :] = pooled
    pooled = _load_halves(pool_refs).astype(bf16)
    y_c = _dot(pooled, w_pool_ref[...]) * pool_scale_ref[...] * _half_silu(block(pt, TILE_COLS, COL_C_Z))
    merged = merged + gated_branch(BR_C, y_c)

    vn = _layer_norm(block(pt, TILE_COLS, COL_B_V), ln_v_g_ref[...], ln_v_b_ref[...]).astype(bf16)
    n_chunks = tile // CHUNK
    lo_group_c = lax.broadcasted_iota(jnp.int32, (CHUNK, LANES), 1) < GROUP_W
    mixed_halves = []
    for h in range(N_HALF):
        vn_wide = jnp.concatenate([vn[c * CHUNK:(c + 1) * CHUNK, h * LANES:(h + 1) * LANES]
                                   for c in range(n_chunks)], axis=1)
        r = _dot(w_s_ref[2 * h * CHUNK:2 * (h + 1) * CHUNK, :], vn_wide)
        mixed_halves.append(jnp.concatenate(
            [jnp.where(lo_group_c, r[0:CHUNK, c * LANES:(c + 1) * LANES], r[CHUNK:2 * CHUNK, c * LANES:(c + 1) * LANES])
             for c in range(n_chunks)], axis=0))
    b_s = jnp.concatenate([b_s_ref[...]] * n_chunks, axis=0)
    mixed = jnp.concatenate(mixed_halves, axis=1) + b_s
    y_b = block(pt, TILE_COLS, COL_B_U) * mixed * _half_silu(block(pt, TILE_COLS, COL_B_Z))
    merged = merged + gated_branch(BR_B, y_b)

    hd = _half_silu(_layer_norm(_load_halves(cd_refs), ln_d_g_ref[...], ln_d_b_ref[...])).astype(bf16)
    y_d = _dot(hd, w_pw_d_ref[...]) * _half_silu(block(pt, TILE_COLS, COL_D_Z))
    merged = (merged + gated_branch(BR_D, y_d)).astype(bf16)

    ln_g = ln_g_ref[...]
    ln_b = ln_b_ref[...]
    for r0 in range(0, tile, OUT_ROWS):
        out = _dot(merged[r0:r0 + OUT_ROWS], w_out_ref[...])
        o_ref[0, r0:r0 + OUT_ROWS] = _layer_norm(ALPHA * x[r0:r0 + OUT_ROWS] + out, ln_g, ln_b)


def _resident(stacked_shape, layer):
    index = (layer,) + (0,) * (len(stacked_shape) - 1)
    return pl.BlockSpec((None,) + tuple(stacked_shape[1:]), lambda b, t: index, pipeline_mode=pl.Buffered(1))


def _tile_rows(n_seq, seq_len):
    fitting = [tile for tile in TILE_CHOICES if seq_len % tile == 0]
    long_enough = [tile for tile in fitting if n_seq * (seq_len // tile) >= MIN_GRID_STEPS]
    return (long_enough or fitting[-1:])[0]


def _encoder_layer(x, weights, layer):
    bn, s, d = x.shape
    tile = _tile_rows(bn, s)
    assert d == D_MODEL and tile % CHUNK == 0 and tile % HALO == 0
    assert HALO % SUBLANES == 0 and HALO >= CONV_D_W // 2 and HALO >= max(POOL_WINDOWS) // 2
    assert tile % (PHASES * PHASE_ROWS) == 0 and tile % OUT_ROWS == 0 and N_HALF * 2 == N_GROUPS
    n_t = s // tile
    halo_per_tile = tile // HALO
    n_halo_blocks = s // HALO
    in_specs = [
        pl.BlockSpec((1, tile, d), lambda b, t: (b, t, 0)),
        pl.BlockSpec((1, HALO, d), lambda b, t: (b, jnp.maximum(t * halo_per_tile - 1, 0), 0)),
        pl.BlockSpec((1, HALO, d), lambda b, t: (b, jnp.minimum((t + 1) * halo_per_tile, n_halo_blocks - 1), 0)),
    ] + [_resident(w.shape, layer) for w in weights]
    halo_scratch = pltpu.VMEM((tile + 2 * HALO, LANES), jnp.float32)
    tile_scratch = pltpu.VMEM((tile, LANES), jnp.float32)
    return pl.pallas_call(
        functools.partial(_layer_kernel, s),
        grid=(bn, n_t),
        in_specs=in_specs,
        out_specs=pl.BlockSpec((1, tile, d), lambda b, t: (b, t, 0)),
        out_shape=jax.ShapeDtypeStruct(x.shape, x.dtype),
        scratch_shapes=[halo_scratch] * (3 * N_HALF) + [tile_scratch] * (3 * N_HALF),
        compiler_params=pltpu.CompilerParams(
            dimension_semantics=("parallel", "parallel"),
            vmem_limit_bytes=VMEM_LIMIT_BYTES),
        name="encoder_layer",
    )(x, x, x, *weights)


def _w_in_prep_kernel(src_block_ref, halve_ref, w_ref, o_ref):
    j = pl.program_id(0)
    scale = jnp.where(halve_ref[j] == 1, 0.5, 1.0)
    o_ref[...] = (w_ref[...] * scale).astype(o_ref.dtype)


def _prepare_w_in(w_in):
    depth, d, in_cols = w_in.shape
    n_blocks = in_cols // BR_W
    src_block = HALO_COLS + TILE_COLS + tuple(range(N_BR_COLS, n_blocks))
    halve = tuple(int(c in HALF_SCALED_COLS or c >= N_BR_COLS) for c in src_block)
    grid_spec = pltpu.PrefetchScalarGridSpec(
        num_scalar_prefetch=2,
        grid=(n_blocks,),
        in_specs=[pl.BlockSpec((depth, d, BR_W), lambda j, src, halve: (0, 0, src[j]))],
        out_specs=pl.BlockSpec((depth, d, BR_W), lambda j, src, halve: (0, 0, j)),
    )
    return pl.pallas_call(
        _w_in_prep_kernel,
        grid_spec=grid_spec,
        out_shape=jax.ShapeDtypeStruct(w_in.shape, jnp.bfloat16),
        compiler_params=pltpu.CompilerParams(dimension_semantics=("parallel",)),
        name="w_in_prep",
    )(jnp.array(src_block, jnp.int32), jnp.array(halve, jnp.int32), w_in)


def _prepare_weights(w_in, conv_a, ln_v_g, ln_v_b, w_s, b_s, w_pool, pool_scale,
                     conv_d, conv_d_b, ln_d_g, ln_d_b, w_pw_d, w_br, w_out, ln_g, ln_b):
    bf16 = jnp.bfloat16
    rows = lambda v: v.reshape(DEPTH, 1, -1)
    b_s_full = jnp.repeat(jnp.swapaxes(b_s, 1, 2), GROUP_W, axis=2)
    eye = jnp.eye(N_GROUPS, dtype=w_pool.dtype)
    w_pool_bd = jnp.einsum("lgij,gh->lgihj", w_pool, eye).reshape(DEPTH, BR_W, BR_W)
    return (
        _prepare_w_in(w_in), conv_a, rows(ln_v_g), rows(ln_v_b),
        w_s.reshape(DEPTH, N_GROUPS * CHUNK, CHUNK).astype(bf16), b_s_full,
        w_pool_bd.astype(bf16), rows(pool_scale), conv_d, rows(conv_d_b),
        0.5 * rows(ln_d_g), 0.5 * rows(ln_d_b),
        w_pw_d.astype(bf16), (0.5 * w_br).astype(bf16), w_out.astype(bf16), rows(ln_g), rows(ln_b),
    )


def kernel(x_prompt, x_sample, w_in, conv_a, ln_v_g, ln_v_b, w_s, b_s, w_pool, pool_scale, conv_d, conv_d_b, ln_d_g, ln_d_b, w_pw_d, w_br, w_out, ln_g, ln_b):
    weights = _prepare_weights(w_in, conv_a, ln_v_g, ln_v_b, w_s, b_s, w_pool, pool_scale, conv_d, conv_d_b,
                               ln_d_g, ln_d_b, w_pw_d, w_br, w_out, ln_g, ln_b)
    y_prompt, y_sample = x_prompt, x_sample
    for layer in range(DEPTH):
        y_prompt = _encoder_layer(y_prompt, weights, layer)
        y_sample = _encoder_layer(y_sample, weights, layer)
    return (y_prompt, y_sample)
```

```python
import functools

import jax
import jax.numpy as jnp
from jax import lax
from jax.experimental import pallas as pl
from jax.experimental.pallas import tpu as pltpu

D_MODEL = 1024
DEPTH = 2
N_BRANCH = 4
BR_W = D_MODEL // N_BRANCH
N_GROUPS = 4
GROUP_W = BR_W // N_GROUPS
CHUNK = 128
POOL_WINDOWS = (2, 4, 8, 16)
CONV_A_W = 3
CONV_D_W = 31
ALPHA = (2 * DEPTH) ** 0.25
LN_EPS = 1e-5
N_BR_COLS = 12

MXU_TILE = 256
LANES = 128
SUBLANES = 8
N_HALF = BR_W // LANES
TILE_CHOICES = (1024, 512)
MIN_GRID_STEPS = 64
HALO = 16
PHASES = 4
PHASE_ROWS = 64
OUT_ROWS = 256
VMEM_LIMIT_BYTES = 60 * 1024 * 1024

(COL_A_H, COL_A_B, COL_A_C, COL_A_Z, COL_B_U, COL_B_V, COL_B_Z, COL_C_P, COL_C_Z,
 COL_D_A, COL_D_G, COL_D_Z) = range(N_BR_COLS)
HALO_COLS = (COL_A_H, COL_A_C, COL_C_P, COL_D_A, COL_D_G, COL_A_B)
TILE_COLS = (COL_A_Z, COL_B_U, COL_B_V, COL_B_Z, COL_C_Z, COL_D_Z)
GATE_COL0 = N_BR_COLS * BR_W
HALF_SCALED_COLS = (COL_A_Z, COL_B_Z, COL_C_Z, COL_D_Z, COL_D_A, COL_D_G)
BR_A, BR_B, BR_C, BR_D = range(N_BRANCH)


def _gate(h):
    return jnp.tanh(h) + 1.0


def _half_silu(h):
    return h * _gate(h)


def _layer_norm(x, g, b):
    mu = jnp.mean(x, axis=-1, keepdims=True)
    xc = x - mu
    var = jnp.mean(xc * xc, axis=-1, keepdims=True)
    return xc * lax.rsqrt(var + LN_EPS) * g + b


def _dot(a, b):
    return jnp.dot(a, b, preferred_element_type=jnp.float32)


def _phase_blocks(tile):
    return [tau + PHASES * m0 for tau in range(PHASES) for m0 in range(0, tile // PHASES, PHASE_ROWS)]


def _phase_rows(ref, row0):
    return ref[pl.ds(row0, PHASE_ROWS, stride=PHASES), :]


def _store_halves(refs, v):
    for h, ref in enumerate(refs):
        ref[...] = v[:, h * LANES:(h + 1) * LANES]


def _load_halves(refs):
    return jnp.concatenate([ref[...] for ref in refs], axis=1)


def _layer_kernel(seq_len, x_ref, xp_ref, xn_ref, w_in_ref, conv_a_ref, ln_v_g_ref, ln_v_b_ref,
                  w_s_ref, b_s_ref, w_pool_ref, pool_scale_ref, conv_d_ref, conv_d_b_ref,
                  ln_d_g_ref, ln_d_b_ref, w_pw_d_ref, w_br_ref, w_out_ref, ln_g_ref, ln_b_ref,
                  o_ref,
                  hh0_ref, hh1_ref, ua0_ref, ua1_ref, p0_ref, p1_ref,
                  cd0_ref, cd1_ref, ca0_ref, ca1_ref, pl0_ref, pl1_ref):
    hh_refs, ua_refs, p_refs = (hh0_ref, hh1_ref), (ua0_ref, ua1_ref), (p0_ref, p1_ref)
    cd_refs, ca_refs, pool_refs = (cd0_ref, cd1_ref), (ca0_ref, ca1_ref), (pl0_ref, pl1_ref)
    t = pl.program_id(1)
    n_t = pl.num_programs(1)
    tile = x_ref.shape[1]
    bf16 = jnp.bfloat16
    f32 = jnp.float32

    x = x_ref[0]
    xp = jnp.where(t > 0, xp_ref[0], 0.0)
    xn = jnp.where(t < n_t - 1, xn_ref[0], 0.0)
    x_bf = x.astype(bf16)
    xh_bf = jnp.concatenate([xp.astype(bf16), x_bf, xn.astype(bf16)], axis=0)

    def block(v, cols, col):
        i = cols.index(col)
        return v[:, i * BR_W:(i + 1) * BR_W]

    def in_proj(lhs, col_blocks):
        cols = []
        for n in col_blocks:
            acc = None
            for k in range(D_MODEL // MXU_TILE):
                part = _dot(lhs[:, k * MXU_TILE:(k + 1) * MXU_TILE], w_in_ref[n, k])
                acc = part if acc is None else acc + part
            cols.append(acc)
        return jnp.concatenate(cols, axis=1)

    def gated_branch(i, y):
        gate_block0 = N_BR_COLS + i * (D_MODEL // BR_W)
        gate = _gate(in_proj(x_bf, range(gate_block0, gate_block0 + D_MODEL // BR_W)))
        return gate * _dot(y.astype(bf16), w_br_ref[i])

    ph = in_proj(xh_bf, range(0, len(HALO_COLS)))
    _store_halves(hh_refs, block(ph, HALO_COLS, COL_D_A) * _gate(block(ph, HALO_COLS, COL_D_G)))
    _store_halves(ua_refs, block(ph, HALO_COLS, COL_A_C) * block(ph, HALO_COLS, COL_A_H))
    _store_halves(p_refs, block(ph, HALO_COLS, COL_C_P))
    a_b = block(ph, HALO_COLS, COL_A_B)[HALO:HALO + tile]
    pt = in_proj(x_bf, range(len(HALO_COLS), N_BR_COLS))

    conv_d = conv_d_ref[...]
    conv_d_b = conv_d_b_ref[...]
    for h in range(N_HALF):
        w_h = conv_d[:, h * LANES:(h + 1) * LANES]
        for m0 in range(0, tile, PHASES * PHASE_ROWS):
            acc = [None] * PHASES
            for off in range(CONV_D_W + PHASES - 1):
                v = _phase_rows(hh_refs[h], HALO - CONV_D_W // 2 + off + m0)
                for tau in range(PHASES):
                    k = off - tau
                    if 0 <= k < CONV_D_W:
                        term = v * w_h[k:k + 1]
                        acc[tau] = term if acc[tau] is None else acc[tau] + term
            for tau in range(PHASES):
                cd_refs[h][pl.ds(tau + m0, PHASE_ROWS, stride=PHASES), :] = acc[tau] + conv_d_b[:, h * LANES:(h + 1) * LANES]

    conv_a = conv_a_ref[...]
    for h in range(N_HALF):
        for row0 in _phase_blocks(tile):
            acc = None
            for k in range(CONV_A_W):
                term = _phase_rows(ua_refs[h], HALO - CONV_A_W // 2 + k + row0) * conv_a[k:k + 1, h * LANES:(h + 1) * LANES]
                acc = term if acc is None else acc + term
            ca_refs[h][pl.ds(row0, PHASE_ROWS, stride=PHASES), :] = acc
    y_a = a_b * _load_halves(ca_refs) * _half_silu(block(pt, TILE_COLS, COL_A_Z))
    merged = gated_branch(BR_A, y_a)

    lo_group = lax.broadcasted_iota(jnp.int32, (PHASE_ROWS, LANES), 1) < GROUP_W
    row_step = PHASES * lax.broadcasted_iota(jnp.int32, (PHASE_ROWS, LANES), 0)
    for h in range(N_HALF):
        half_lo, half_hi = POOL_WINDOWS[2 * h] // 2, POOL_WINDOWS[2 * h + 1] // 2
        half = jnp.where(lo_group, half_lo, half_hi)
        for row0 in _phase_blocks(tile):
            def window(d0, d1):
                s = None
                for d in range(d0, d1):
                    v = _phase_rows(p_refs[h], HALO + d + row0)
                    s = v if s is None else s + v
                return s
            inner = window(-half_lo, half_lo)
            outer = inner + window(-half_hi, -half_lo) + window(half_lo, half_hi)
            pos = t * tile + row0 + row_step
            cnt = (jnp.minimum(pos + half, seq_len) - jnp.maximum(pos - half, 0)).astype(f32)
            pooled = jnp.where(lo_group, inner, outer) / cnt - _phase_rows(p_refs[h], HALO + row0)
            pool_refs[h][pl.ds(row0, PHASE_ROWS, stride=PHASES), :] = pooled
    pooled = _load_halves(pool_refs).astype(bf16)
    y_c = _dot(pooled, w_pool_ref[...]) * pool_scale_ref[...] * _half_silu(block(pt, TILE_COLS, COL_C_Z))
    merged = merged + gated_branch(BR_C, y_c)

    vn = _layer_norm(block(pt, TILE_COLS, COL_B_V), ln_v_g_ref[...], ln_v_b_ref[...]).astype(bf16)
    n_chunks = tile // CHUNK
    lo_group_c = lax.broadcasted_iota(jnp.int32, (CHUNK, LANES), 1) < GROUP_W
    mixed_halves = []
    for h in range(N_HALF):
        vn_wide = jnp.concatenate([vn[c * CHUNK:(c + 1) * CHUNK, h * LANES:(h + 1) * LANES]
                                   for c in range(n_chunks)], axis=1)
        r = _dot(w_s_ref[2 * h * CHUNK:2 * (h + 1) * CHUNK, :], vn_wide)
        mixed_halves.append(jnp.concatenate(
            [jnp.where(lo_group_c, r[0:CHUNK, c * LANES:(c + 1) * LANES], r[CHUNK:2 * CHUNK, c * LANES:(c + 1) * LANES])
             for c in range(n_chunks)], axis=0))
    b_s = jnp.concatenate([b_s_ref[...]] * n_chunks, axis=0)
    mixed = jnp.concatenate(mixed_halves, axis=1) + b_s
    y_b = block(pt, TILE_COLS, COL_B_U) * mixed * _half_silu(block(pt, TILE_COLS, COL_B_Z))
    merged = merged + gated_branch(BR_B, y_b)

    hd = _half_silu(_layer_norm(_load_halves(cd_refs), ln_d_g_ref[...], ln_d_b_ref[...])).astype(bf16)
    y_d = _dot(hd, w_pw_d_ref[...]) * _half_silu(block(pt, TILE_COLS, COL_D_Z))
    merged = (merged + gated_branch(BR_D, y_d)).astype(bf16)

    ln_g = ln_g_ref[...]
    ln_b = ln_b_ref[...]
    for r0 in range(0, tile, OUT_ROWS):
        out = _dot(merged[r0:r0 + OUT_ROWS], w_out_ref[...])
        o_ref[0, r0:r0 + OUT_ROWS] = _layer_norm(ALPHA * x[r0:r0 + OUT_ROWS] + out, ln_g, ln_b)


def _resident(stacked_shape, layer):
    index = (layer,) + (0,) * (len(stacked_shape) - 1)
    return pl.BlockSpec((None,) + tuple(stacked_shape[1:]), lambda b, t: index, pipeline_mode=pl.Buffered(1))


def _tile_rows(n_seq, seq_len):
    fitting = [tile for tile in TILE_CHOICES if seq_len % tile == 0]
    long_enough = [tile for tile in fitting if n_seq * (seq_len // tile) >= MIN_GRID_STEPS]
    return (long_enough or fitting[-1:])[0]


def _encoder_layer(x, weights, layer):
    bn, s, d = x.shape
    tile = _tile_rows(bn, s)
    assert d == D_MODEL and tile % CHUNK == 0 and tile % HALO == 0
    assert HALO % SUBLANES == 0 and HALO >= CONV_D_W // 2 and HALO >= max(POOL_WINDOWS) // 2
    assert tile % (PHASES * PHASE_ROWS) == 0 and tile % OUT_ROWS == 0 and N_HALF * 2 == N_GROUPS
    n_t = s // tile
    halo_per_tile = tile // HALO
    n_halo_blocks = s // HALO
    in_specs = [
        pl.BlockSpec((1, tile, d), lambda b, t: (b, t, 0)),
        pl.BlockSpec((1, HALO, d), lambda b, t: (b, jnp.maximum(t * halo_per_tile - 1, 0), 0)),
        pl.BlockSpec((1, HALO, d), lambda b, t: (b, jnp.minimum((t + 1) * halo_per_tile, n_halo_blocks - 1), 0)),
    ] + [_resident(w.shape, layer) for w in weights]
    halo_scratch = pltpu.VMEM((tile + 2 * HALO, LANES), jnp.float32)
    tile_scratch = pltpu.VMEM((tile, LANES), jnp.float32)
    return pl.pallas_call(
        functools.partial(_layer_kernel, s),
        grid=(bn, n_t),
        in_specs=in_specs,
        out_specs=pl.BlockSpec((1, tile, d), lambda b, t: (b, t, 0)),
        out_shape=jax.ShapeDtypeStruct(x.shape, x.dtype),
        scratch_shapes=[halo_scratch] * (3 * N_HALF) + [tile_scratch] * (3 * N_HALF),
        compiler_params=pltpu.CompilerParams(
            dimension_semantics=("parallel", "parallel"),
            vmem_limit_bytes=VMEM_LIMIT_BYTES),
        name="encoder_layer",
    )(x, x, x, *weights)


def _w_in_prep_kernel(src_block_ref, halve_ref, w_ref, o_ref):
    j = pl.program_id(0)
    scale = jnp.where(halve_ref[j] == 1, 0.5, 1.0)
    for k in range(o_ref.shape[2]):
        o_ref[:, 0, k] = (w_ref[:, k * MXU_TILE:(k + 1) * MXU_TILE, :] * scale).astype(o_ref.dtype)


def _prepare_w_in(w_in):
    depth, d, in_cols = w_in.shape
    n_blocks = in_cols // BR_W
    src_block = HALO_COLS + TILE_COLS + tuple(range(N_BR_COLS, n_blocks))
    halve = tuple(int(c in HALF_SCALED_COLS or c >= N_BR_COLS) for c in src_block)
    grid_spec = pltpu.PrefetchScalarGridSpec(
        num_scalar_prefetch=2,
        grid=(n_blocks,),
        in_specs=[pl.BlockSpec((depth, d, BR_W), lambda j, src, halve: (0, 0, src[j]))],
        out_specs=pl.BlockSpec((depth, 1, d // MXU_TILE, MXU_TILE, BR_W), lambda j, src, halve: (0, j, 0, 0, 0)),
    )
    return pl.pallas_call(
        _w_in_prep_kernel,
        grid_spec=grid_spec,
        out_shape=jax.ShapeDtypeStruct((depth, n_blocks, d // MXU_TILE, MXU_TILE, BR_W), jnp.bfloat16),
        compiler_params=pltpu.CompilerParams(dimension_semantics=("parallel",)),
        name="w_in_prep",
    )(jnp.array(src_block, jnp.int32), jnp.array(halve, jnp.int32), w_in)


def _prepare_weights(w_in, conv_a, ln_v_g, ln_v_b, w_s, b_s, w_pool, pool_scale,
                     conv_d, conv_d_b, ln_d_g, ln_d_b, w_pw_d, w_br, w_out, ln_g, ln_b):
    bf16 = jnp.bfloat16
    rows = lambda v: v.reshape(DEPTH, 1, -1)
    b_s_full = jnp.repeat(jnp.swapaxes(b_s, 1, 2), GROUP_W, axis=2)
    eye = jnp.eye(N_GROUPS, dtype=w_pool.dtype)
    w_pool_bd = jnp.einsum("lgij,gh->lgihj", w_pool, eye).reshape(DEPTH, BR_W, BR_W)
    return (
        _prepare_w_in(w_in), conv_a, rows(ln_v_g), rows(ln_v_b),
        w_s.reshape(DEPTH, N_GROUPS * CHUNK, CHUNK).astype(bf16), b_s_full,
        w_pool_bd.astype(bf16), rows(pool_scale), conv_d, rows(conv_d_b),
        0.5 * rows(ln_d_g), 0.5 * rows(ln_d_b),
        w_pw_d.astype(bf16), (0.5 * w_br).astype(bf16), w_out.astype(bf16), rows(ln_g), rows(ln_b),
    )


def kernel(x_prompt, x_sample, w_in, conv_a, ln_v_g, ln_v_b, w_s, b_s, w_pool, pool_scale, conv_d, conv_d_b, ln_d_g, ln_d_b, w_pw_d, w_br, w_out, ln_g, ln_b):
    weights = _prepare_weights(w_in, conv_a, ln_v_g, ln_v_b, w_s, b_s, w_pool, pool_scale, conv_d, conv_d_b,
                               ln_d_g, ln_d_b, w_pw_d, w_br, w_out, ln_g, ln_b)
    y_prompt, y_sample = x_prompt, x_sample
    for layer in range(DEPTH):
        y_prompt = _encoder_layer(y_prompt, weights, layer)
        y_sample = _encoder_layer(y_sample, weights, layer)
    return (y_prompt, y_sample)
```

```python
import functools

import jax
import jax.numpy as jnp
from jax import lax
from jax.experimental import pallas as pl
from jax.experimental.pallas import tpu as pltpu

D_MODEL = 1024
DEPTH = 2
N_BRANCH = 4
BR_W = D_MODEL // N_BRANCH
N_GROUPS = 4
GROUP_W = BR_W // N_GROUPS
CHUNK = 128
POOL_WINDOWS = (2, 4, 8, 16)
CONV_A_W = 3
CONV_D_W = 31
ALPHA = (2 * DEPTH) ** 0.25
LN_EPS = 1e-5
N_BR_COLS = 12

MXU_TILE = 256
LANES = 128
SUBLANES = 8
N_HALF = BR_W // LANES
TILE_CHOICES = (1024, 512)
MIN_GRID_STEPS = 64
HALO = 16
PHASES = 4
PHASE_ROWS = 64
OUT_ROWS = 256
VMEM_LIMIT_BYTES = 60 * 1024 * 1024

(COL_A_H, COL_A_B, COL_A_C, COL_A_Z, COL_B_U, COL_B_V, COL_B_Z, COL_C_P, COL_C_Z,
 COL_D_A, COL_D_G, COL_D_Z) = range(N_BR_COLS)
HALO_COLS = (COL_A_H, COL_A_C, COL_C_P, COL_D_A, COL_D_G, COL_A_B)
TILE_COLS = (COL_A_Z, COL_B_U, COL_B_V, COL_B_Z, COL_C_Z, COL_D_Z)
HALF_SCALED_COLS = (COL_A_Z, COL_B_Z, COL_C_Z, COL_D_Z, COL_D_A, COL_D_G)
BR_A, BR_B, BR_C, BR_D = range(N_BRANCH)


def _gate(h):
    return jnp.tanh(h) + 1.0


def _half_silu(h):
    return h * _gate(h)


def _layer_norm(x, g, b):
    mu = jnp.mean(x, axis=-1, keepdims=True)
    xc = x - mu
    var = jnp.mean(xc * xc, axis=-1, keepdims=True)
    return xc * lax.rsqrt(var + LN_EPS) * g + b


def _dot(a, b):
    return jnp.dot(a, b, preferred_element_type=jnp.float32)


def _phase_blocks(tile):
    return [tau + PHASES * m0 for tau in range(PHASES) for m0 in range(0, tile // PHASES, PHASE_ROWS)]


def _phase_rows(ref, row0):
    return ref[pl.ds(row0, PHASE_ROWS, stride=PHASES), :]


def _store_halves(refs, v):
    for h, ref in enumerate(refs):
        ref[...] = v[:, h * LANES:(h + 1) * LANES]


def _load_halves(refs):
    return jnp.concatenate([ref[...] for ref in refs], axis=1)


def _layer_kernel(seq_len, x_ref, xp_ref, xn_ref, w_in_ref, conv_a_ref, ln_v_g_ref, ln_v_b_ref,
                  w_s_ref, b_s_ref, w_pool_ref, pool_scale_ref, conv_d_ref, conv_d_b_ref,
                  ln_d_g_ref, ln_d_b_ref, w_pw_d_ref, w_br_ref, w_out_ref, ln_g_ref, ln_b_ref,
                  o_ref,
                  hh0_ref, hh1_ref, ua0_ref, ua1_ref, p0_ref, p1_ref,
                  cd0_ref, cd1_ref, ca0_ref, ca1_ref, pl0_ref, pl1_ref):
    hh_refs, ua_refs, p_refs = (hh0_ref, hh1_ref), (ua0_ref, ua1_ref), (p0_ref, p1_ref)
    cd_refs, ca_refs, pool_refs = (cd0_ref, cd1_ref), (ca0_ref, ca1_ref), (pl0_ref, pl1_ref)
    t = pl.program_id(1)
    n_t = pl.num_programs(1)
    tile = x_ref.shape[1]
    bf16 = jnp.bfloat16
    f32 = jnp.float32

    x = x_ref[0]
    xp = jnp.where(t > 0, xp_ref[0], 0.0)
    xn = jnp.where(t < n_t - 1, xn_ref[0], 0.0)
    x_bf = x.astype(bf16)
    xh_bf = jnp.concatenate([xp.astype(bf16), x_bf, xn.astype(bf16)], axis=0)

    def block(v, cols, col):
        i = cols.index(col)
        return v[:, i * BR_W:(i + 1) * BR_W]

    def in_proj(lhs, col_blocks):
        cols = []
        for n in col_blocks:
            acc = None
            for k in range(D_MODEL // MXU_TILE):
                part = _dot(lhs[:, k * MXU_TILE:(k + 1) * MXU_TILE], w_in_ref[n, k])
                acc = part if acc is None else acc + part
            cols.append(acc)
        return jnp.concatenate(cols, axis=1)

    def gated_branch(i, y):
        gate_block0 = N_BR_COLS + i * (D_MODEL // BR_W)
        gate = _gate(in_proj(x_bf, range(gate_block0, gate_block0 + D_MODEL // BR_W)))
        return gate * _dot(y.astype(bf16), w_br_ref[i])

    ph = in_proj(xh_bf, range(0, len(HALO_COLS)))
    _store_halves(hh_refs, block(ph, HALO_COLS, COL_D_A) * _gate(block(ph, HALO_COLS, COL_D_G)))
    _store_halves(ua_refs, block(ph, HALO_COLS, COL_A_C) * block(ph, HALO_COLS, COL_A_H))
    _store_halves(p_refs, block(ph, HALO_COLS, COL_C_P))
    a_b = block(ph, HALO_COLS, COL_A_B)[HALO:HALO + tile]
    pt = in_proj(x_bf, range(len(HALO_COLS), N_BR_COLS))

    conv_d = conv_d_ref[...]
    conv_d_b = conv_d_b_ref[...]
    for h in range(N_HALF):
        w_h = conv_d[:, h * LANES:(h + 1) * LANES]
        for m0 in range(0, tile, PHASES * PHASE_ROWS):
            acc = [None] * PHASES
            for off in range(CONV_D_W + PHASES - 1):
                v = _phase_rows(hh_refs[h], HALO - CONV_D_W // 2 + off + m0)
                for tau in range(PHASES):
                    k = off - tau
                    if 0 <= k < CONV_D_W:
                        term = v * w_h[k:k + 1]
                        acc[tau] = term if acc[tau] is None else acc[tau] + term
            for tau in range(PHASES):
                cd_refs[h][pl.ds(tau + m0, PHASE_ROWS, stride=PHASES), :] = acc[tau] + conv_d_b[:, h * LANES:(h + 1) * LANES]

    conv_a = conv_a_ref[...]
    for h in range(N_HALF):
        for row0 in _phase_blocks(tile):
            acc = None
            for k in range(CONV_A_W):
                term = _phase_rows(ua_refs[h], HALO - CONV_A_W // 2 + k + row0) * conv_a[k:k + 1, h * LANES:(h + 1) * LANES]
                acc = term if acc is None else acc + term
            ca_refs[h][pl.ds(row0, PHASE_ROWS, stride=PHASES), :] = acc
    y_a = a_b * _load_halves(ca_refs) * _half_silu(block(pt, TILE_COLS, COL_A_Z))
    merged = gated_branch(BR_A, y_a)

    lo_group = lax.broadcasted_iota(jnp.int32, (PHASE_ROWS, LANES), 1) < GROUP_W
    row_step = PHASES * lax.broadcasted_iota(jnp.int32, (PHASE_ROWS, LANES), 0)
    for h in range(N_HALF):
        half_lo, half_hi = POOL_WINDOWS[2 * h] // 2, POOL_WINDOWS[2 * h + 1] // 2
        half = jnp.where(lo_group, half_lo, half_hi)
        for row0 in _phase_blocks(tile):
            def window(d0, d1):
                s = None
                for d in range(d0, d1):
                    v = _phase_rows(p_refs[h], HALO + d + row0)
                    s = v if s is None else s + v
                return s
            inner = window(-half_lo, half_lo)
            outer = inner + window(-half_hi, -half_lo) + window(half_lo, half_hi)
            pos = t * tile + row0 + row_step
            cnt = (jnp.minimum(pos + half, seq_len) - jnp.maximum(pos - half, 0)).astype(f32)
            pooled = jnp.where(lo_group, inner, outer) / cnt - _phase_rows(p_refs[h], HALO + row0)
            pool_refs[h][pl.ds(row0, PHASE_ROWS, stride=PHASES), :] = pooled
    pooled = _load_halves(pool_refs).astype(bf16)
    y_c = _dot(pooled, w_pool_ref[...]) * pool_scale_ref[...] * _half_silu(block(pt, TILE_COLS, COL_C_Z))
    merged = merged + gated_branch(BR_C, y_c)

    vn = _layer_norm(block(pt, TILE_COLS, COL_B_V), ln_v_g_ref[...], ln_v_b_ref[...]).astype(bf16)
    n_chunks = tile // CHUNK
    lo_group_c = lax.broadcasted_iota(jnp.int32, (CHUNK, LANES), 1) < GROUP_W
    mixed_halves = []
    for h in range(N_HALF):
        vn_wide = jnp.concatenate([vn[c * CHUNK:(c + 1) * CHUNK, h * LANES:(h + 1) * LANES]
                                   for c in range(n_chunks)], axis=1)
        r = _dot(w_s_ref[2 * h * CHUNK:2 * (h + 1) * CHUNK, :], vn_wide)
        mixed_halves.append(jnp.concatenate(
            [jnp.where(lo_group_c, r[0:CHUNK, c * LANES:(c + 1) * LANES], r[CHUNK:2 * CHUNK, c * LANES:(c + 1) * LANES])
             for c in range(n_chunks)], axis=0))
    b_s = jnp.concatenate([b_s_ref[...]] * n_chunks, axis=0)
    mixed = jnp.concatenate(mixed_halves, axis=1) + b_s
    y_b = block(pt, TILE_COLS, COL_B_U) * mixed * _half_silu(block(pt, TILE_COLS, COL_B_Z))
    merged = merged + gated_branch(BR_B, y_b)

    hd = _half_silu(_layer_norm(_load_halves(cd_refs), ln_d_g_ref[...], ln_d_b_ref[...])).astype(bf16)
    y_d = _dot(hd, w_pw_d_ref[...]) * _half_silu(block(pt, TILE_COLS, COL_D_Z))
    merged = (merged + gated_branch(BR_D, y_d)).astype(bf16)

    ln_g = ln_g_ref[...]
    ln_b = ln_b_ref[...]
    for r0 in range(0, tile, OUT_ROWS):
        out = _dot(merged[r0:r0 + OUT_ROWS], w_out_ref[...])
        o_ref[0, r0:r0 + OUT_ROWS] = _layer_norm(ALPHA * x[r0:r0 + OUT_ROWS] + out, ln_g, ln_b)


def _resident(stacked_shape, layer):
    index = (layer,) + (0,) * (len(stacked_shape) - 1)
    return pl.BlockSpec((None,) + tuple(stacked_shape[1:]), lambda b, t: index, pipeline_mode=pl.Buffered(1))


def _tile_rows(n_seq, seq_len):
    fitting = [tile for tile in TILE_CHOICES if seq_len % tile == 0]
    long_enough = [tile for tile in fitting if n_seq * (seq_len // tile) >= MIN_GRID_STEPS]
    return (long_enough or fitting[-1:])[0]


def _encoder_layer(x, weights, layer):
    bn, s, d = x.shape
    tile = _tile_rows(bn, s)
    assert d == D_MODEL and tile % CHUNK == 0 and tile % HALO == 0
    assert HALO % SUBLANES == 0 and HALO >= CONV_D_W // 2 and HALO >= max(POOL_WINDOWS) // 2
    assert tile % (PHASES * PHASE_ROWS) == 0 and tile % OUT_ROWS == 0 and N_HALF * 2 == N_GROUPS
    n_t = s // tile
    halo_per_tile = tile // HALO
    n_halo_blocks = s // HALO
    in_specs = [
        pl.BlockSpec((1, tile, d), lambda b, t: (b, t, 0)),
        pl.BlockSpec((1, HALO, d), lambda b, t: (b, jnp.maximum(t * halo_per_tile - 1, 0), 0)),
        pl.BlockSpec((1, HALO, d), lambda b, t: (b, jnp.minimum((t + 1) * halo_per_tile, n_halo_blocks - 1), 0)),
    ] + [_resident(w.shape, layer) for w in weights]
    halo_scratch = pltpu.VMEM((tile + 2 * HALO, LANES), jnp.float32)
    tile_scratch = pltpu.VMEM((tile, LANES), jnp.float32)
    return pl.pallas_call(
        functools.partial(_layer_kernel, s),
        grid=(bn, n_t),
        in_specs=in_specs,
        out_specs=pl.BlockSpec((1, tile, d), lambda b, t: (b, t, 0)),
        out_shape=jax.ShapeDtypeStruct(x.shape, x.dtype),
        scratch_shapes=[halo_scratch] * (3 * N_HALF) + [tile_scratch] * (3 * N_HALF),
        compiler_params=pltpu.CompilerParams(
            dimension_semantics=("parallel", "parallel"),
            vmem_limit_bytes=VMEM_LIMIT_BYTES),
        name="encoder_layer",
    )(x, x, x, *weights)


def _w_in_prep_kernel(src_block_ref, halve_ref, w_ref, o_ref):
    j = pl.program_id(0)
    scale = jnp.where(halve_ref[j] == 1, 0.5, 1.0)
    for k in range(o_ref.shape[2]):
        o_ref[:, 0, k] = (w_ref[:, k * MXU_TILE:(k + 1) * MXU_TILE, :] * scale).astype(o_ref.dtype)


def _prepare_w_in(w_in):
    depth, d, in_cols = w_in.shape
    n_blocks = in_cols // BR_W
    src_block = HALO_COLS + TILE_COLS + tuple(range(N_BR_COLS, n_blocks))
    halve = tuple(int(c in HALF_SCALED_COLS or c >= N_BR_COLS) for c in src_block)
    grid_spec = pltpu.PrefetchScalarGridSpec(
        num_scalar_prefetch=2,
        grid=(n_blocks,),
        in_specs=[pl.BlockSpec((depth, d, BR_W), lambda j, src, halve: (0, 0, src[j]))],
        out_specs=pl.BlockSpec((depth, 1, d // MXU_TILE, MXU_TILE, BR_W), lambda j, src, halve: (0, j, 0, 0, 0)),
    )
    return pl.pallas_call(
        _w_in_prep_kernel,
        grid_spec=grid_spec,
        out_shape=jax.ShapeDtypeStruct((depth, n_blocks, d // MXU_TILE, MXU_TILE, BR_W), jnp.bfloat16),
        compiler_params=pltpu.CompilerParams(dimension_semantics=("parallel",)),
        name="w_in_prep",
    )(jnp.array(src_block, jnp.int32), jnp.array(halve, jnp.int32), w_in)


def _prepare_weights(w_in, conv_a, ln_v_g, ln_v_b, w_s, b_s, w_pool, pool_scale,
                     conv_d, conv_d_b, ln_d_g, ln_d_b, w_pw_d, w_br, w_out, ln_g, ln_b):
    bf16 = jnp.bfloat16
    rows = lambda v: v.reshape(DEPTH, 1, -1)
    b_s_full = jnp.repeat(jnp.swapaxes(b_s, 1, 2), GROUP_W, axis=2)
    eye = jnp.eye(N_GROUPS, dtype=w_pool.dtype)
    w_pool_bd = jnp.einsum("lgij,gh->lgihj", w_pool, eye).reshape(DEPTH, BR_W, BR_W)
    return (
        _prepare_w_in(w_in), conv_a, rows(ln_v_g), rows(ln_v_b),
        w_s.reshape(DEPTH, N_GROUPS * CHUNK, CHUNK).astype(bf16), b_s_full,
        w_pool_bd.astype(bf16), rows(pool_scale), conv_d, rows(conv_d_b),
        0.5 * rows(ln_d_g), 0.5 * rows(ln_d_b),
        w_pw_d.astype(bf16), (0.5 * w_br).astype(bf16), w_out.astype(bf16), rows(ln_g), rows(ln_b),
    )


def kernel(x_prompt, x_sample, w_in, conv_a, ln_v_g, ln_v_b, w_s, b_s, w_pool, pool_scale, conv_d, conv_d_b, ln_d_g, ln_d_b, w_pw_d, w_br, w_out, ln_g, ln_b):
    weights = _prepare_weights(w_in, conv_a, ln_v_g, ln_v_b, w_s, b_s, w_pool, pool_scale, conv_d, conv_d_b,
                               ln_d_g, ln_d_b, w_pw_d, w_br, w_out, ln_g, ln_b)
    y_prompt, y_sample = x_prompt, x_sample
    for layer in range(DEPTH):
        y_prompt = _encoder_layer(y_prompt, weights, layer)
        y_sample = _encoder_layer(y_sample, weights, layer)
    return (y_prompt, y_sample)
```

```python
import functools

import jax
import jax.numpy as jnp
from jax import lax
from jax.experimental import pallas as pl
from jax.experimental.pallas import tpu as pltpu

D_MODEL = 1024
DEPTH = 2
N_BRANCH = 4
BR_W = D_MODEL // N_BRANCH
N_GROUPS = 4
GROUP_W = BR_W // N_GROUPS
CHUNK = 128
POOL_WINDOWS = (2, 4, 8, 16)
CONV_A_W = 3
CONV_D_W = 31
ALPHA = (2 * DEPTH) ** 0.25
LN_EPS = 1e-5
N_BR_COLS = 12

MXU_TILE = 256
LANES = 128
SUBLANES = 8
N_HALF = BR_W // LANES
TILE_CHOICES = (1024, 512)
MIN_GRID_STEPS = 64
HALO = 16
PHASES = 4
PHASE_ROWS = 64
OUT_ROWS = 256
VMEM_LIMIT_BYTES = 60 * 1024 * 1024

(COL_A_H, COL_A_B, COL_A_C, COL_A_Z, COL_B_U, COL_B_V, COL_B_Z, COL_C_P, COL_C_Z,
 COL_D_A, COL_D_G, COL_D_Z) = range(N_BR_COLS)
HALO_COLS = (COL_A_H, COL_A_C, COL_C_P, COL_D_A, COL_D_G, COL_A_B)
TILE_COLS = (COL_A_Z, COL_B_U, COL_B_V, COL_B_Z, COL_C_Z, COL_D_Z)
HALF_SCALED_COLS = (COL_A_Z, COL_B_Z, COL_C_Z, COL_D_Z, COL_D_A, COL_D_G)
BR_A, BR_B, BR_C, BR_D = range(N_BRANCH)


def _gate(h):
    return jnp.tanh(h) + 1.0


def _half_silu(h):
    return h * _gate(h)


def _layer_norm(x, g, b):
    mu = jnp.mean(x, axis=-1, keepdims=True)
    xc = x - mu
    var = jnp.mean(xc * xc, axis=-1, keepdims=True)
    return xc * lax.rsqrt(var + LN_EPS) * g + b


def _dot(a, b):
    return jnp.dot(a, b, preferred_element_type=jnp.float32)


def _phase_blocks(tile):
    return [tau + PHASES * m0 for tau in range(PHASES) for m0 in range(0, tile // PHASES, PHASE_ROWS)]


def _phase_rows(ref, row0):
    return ref[pl.ds(row0, PHASE_ROWS, stride=PHASES), :]


def _store_halves(refs, v):
    for h, ref in enumerate(refs):
        ref[...] = v[:, h * LANES:(h + 1) * LANES]


def _load_halves(refs):
    return jnp.concatenate([ref[...] for ref in refs], axis=1)


def _layer_kernel(seq_len, x_ref, xp_ref, xn_ref, w_in_ref, conv_a_ref, ln_v_g_ref, ln_v_b_ref,
                  w_s_ref, b_s_ref, w_pool_ref, pool_scale_ref, conv_d_ref, conv_d_b_ref,
                  ln_d_g_ref, ln_d_b_ref, w_pw_d_ref, w_br_ref, w_out_ref, ln_g_ref, ln_b_ref,
                  o_ref,
                  hh0_ref, hh1_ref, ua0_ref, ua1_ref, p0_ref, p1_ref,
                  cd0_ref, cd1_ref, ca0_ref, ca1_ref, pl0_ref, pl1_ref):
    hh_refs, ua_refs, p_refs = (hh0_ref, hh1_ref), (ua0_ref, ua1_ref), (p0_ref, p1_ref)
    cd_refs, ca_refs, pool_refs = (cd0_ref, cd1_ref), (ca0_ref, ca1_ref), (pl0_ref, pl1_ref)
    t = pl.program_id(1)
    n_t = pl.num_programs(1)
    tile = x_ref.shape[1]
    bf16 = jnp.bfloat16
    f32 = jnp.float32

    x = x_ref[0]
    xp = jnp.where(t > 0, xp_ref[0], 0.0)
    xn = jnp.where(t < n_t - 1, xn_ref[0], 0.0)
    x_bf = x.astype(bf16)
    xh_bf = jnp.concatenate([xp.astype(bf16), x_bf, xn.astype(bf16)], axis=0)

    def block(v, cols, col):
        i = cols.index(col)
        return v[:, i * BR_W:(i + 1) * BR_W]

    def in_proj(lhs, col_blocks):
        cols = []
        for n in col_blocks:
            acc = None
            for k in range(D_MODEL // MXU_TILE):
                part = _dot(lhs[:, k * MXU_TILE:(k + 1) * MXU_TILE], w_in_ref[n, k])
                acc = part if acc is None else acc + part
            cols.append(acc)
        return jnp.concatenate(cols, axis=1)

    def gated_branch(i, y):
        gate_block0 = N_BR_COLS + i * (D_MODEL // BR_W)
        gate = _gate(in_proj(x_bf, range(gate_block0, gate_block0 + D_MODEL // BR_W)))
        return gate * _dot(y.astype(bf16), w_br_ref[i])

    ph = in_proj(xh_bf, range(0, len(HALO_COLS)))
    _store_halves(hh_refs, block(ph, HALO_COLS, COL_D_A) * _gate(block(ph, HALO_COLS, COL_D_G)))
    _store_halves(ua_refs, block(ph, HALO_COLS, COL_A_C) * block(ph, HALO_COLS, COL_A_H))
    _store_halves(p_refs, block(ph, HALO_COLS, COL_C_P))
    a_b = block(ph, HALO_COLS, COL_A_B)[HALO:HALO + tile]
    pt = in_proj(x_bf, range(len(HALO_COLS), N_BR_COLS))

    conv_d = conv_d_ref[...]
    conv_d_b = conv_d_b_ref[...]
    for h in range(N_HALF):
        w_h = conv_d[:, h * LANES:(h + 1) * LANES]
        for m0 in range(0, tile, PHASES * PHASE_ROWS):
            acc = [None] * PHASES
            for off in range(CONV_D_W + PHASES - 1):
                v = _phase_rows(hh_refs[h], HALO - CONV_D_W // 2 + off + m0)
                for tau in range(PHASES):
                    k = off - tau
                    if 0 <= k < CONV_D_W:
                        term = v * w_h[k:k + 1]
                        acc[tau] = term if acc[tau] is None else acc[tau] + term
            for tau in range(PHASES):
                cd_refs[h][pl.ds(tau + m0, PHASE_ROWS, stride=PHASES), :] = acc[tau] + conv_d_b[:, h * LANES:(h + 1) * LANES]

    conv_a = conv_a_ref[...]
    for h in range(N_HALF):
        for row0 in _phase_blocks(tile):
            acc = None
            for k in range(CONV_A_W):
                term = _phase_rows(ua_refs[h], HALO - CONV_A_W // 2 + k + row0) * conv_a[k:k + 1, h * LANES:(h + 1) * LANES]
                acc = term if acc is None else acc + term
            ca_refs[h][pl.ds(row0, PHASE_ROWS, stride=PHASES), :] = acc
    y_a = a_b * _load_halves(ca_refs) * _half_silu(block(pt, TILE_COLS, COL_A_Z))
    merged = gated_branch(BR_A, y_a)

    lo_group = lax.broadcasted_iota(jnp.int32, (PHASE_ROWS, LANES), 1) < GROUP_W
    row_step = PHASES * lax.broadcasted_iota(jnp.int32, (PHASE_ROWS, LANES), 0)
    for h in range(N_HALF):
        half_lo, half_hi = POOL_WINDOWS[2 * h] // 2, POOL_WINDOWS[2 * h + 1] // 2
        half = jnp.where(lo_group, half_lo, half_hi)
        for row0 in _phase_blocks(tile):
            def window(d0, d1):
                s = None
                for d in range(d0, d1):
                    v = _phase_rows(p_refs[h], HALO + d + row0)
                    s = v if s is None else s + v
                return s
            inner = window(-half_lo, half_lo)
            outer = inner + window(-half_hi, -half_lo) + window(half_lo, half_hi)
            pos = t * tile + row0 + row_step
            cnt = (jnp.minimum(pos + half, seq_len) - jnp.maximum(pos - half, 0)).astype(f32)
            pooled = jnp.where(lo_group, inner, outer) / cnt - _phase_rows(p_refs[h], HALO + row0)
            pool_refs[h][pl.ds(row0, PHASE_ROWS, stride=PHASES), :] = pooled
    pooled = _load_halves(pool_refs).astype(bf16)
    y_c = _dot(pooled, w_pool_ref[...]) * pool_scale_ref[...] * _half_silu(block(pt, TILE_COLS, COL_C_Z))
    merged = merged + gated_branch(BR_C, y_c)

    vn = _layer_norm(block(pt, TILE_COLS, COL_B_V), ln_v_g_ref[...], ln_v_b_ref[...]).astype(bf16)
    n_chunks = tile // CHUNK
    lo_group_c = lax.broadcasted_iota(jnp.int32, (CHUNK, LANES), 1) < GROUP_W
    mixed_halves = []
    for h in range(N_HALF):
        vn_wide = jnp.concatenate([vn[c * CHUNK:(c + 1) * CHUNK, h * LANES:(h + 1) * LANES]
                                   for c in range(n_chunks)], axis=1)
        r = _dot(w_s_ref[2 * h * CHUNK:2 * (h + 1) * CHUNK, :], vn_wide)
        mixed_halves.append(jnp.concatenate(
            [jnp.where(lo_group_c, r[0:CHUNK, c * LANES:(c + 1) * LANES], r[CHUNK:2 * CHUNK, c * LANES:(c + 1) * LANES])
             for c in range(n_chunks)], axis=0))
    b_s = jnp.concatenate([b_s_ref[...]] * n_chunks, axis=0)
    mixed = jnp.concatenate(mixed_halves, axis=1) + b_s
    y_b = block(pt, TILE_COLS, COL_B_U) * mixed * _half_silu(block(pt, TILE_COLS, COL_B_Z))
    merged = merged + gated_branch(BR_B, y_b)

    hd = _half_silu(_layer_norm(_load_halves(cd_refs), ln_d_g_ref[...], ln_d_b_ref[...])).astype(bf16)
    y_d = _dot(hd, w_pw_d_ref[...]) * _half_silu(block(pt, TILE_COLS, COL_D_Z))
    merged = (merged + gated_branch(BR_D, y_d)).astype(bf16)

    ln_g = ln_g_ref[...]
    ln_b = ln_b_ref[...]
    for r0 in range(0, tile, OUT_ROWS):
        out = _dot(merged[r0:r0 + OUT_ROWS], w_out_ref[...])
        o_ref[0, r0:r0 + OUT_ROWS] = _layer_norm(ALPHA * x_ref[0, r0:r0 + OUT_ROWS] + out, ln_g, ln_b)


def _resident(stacked_shape, layer):
    index = (layer,) + (0,) * (len(stacked_shape) - 1)
    return pl.BlockSpec((None,) + tuple(stacked_shape[1:]), lambda b, t: index, pipeline_mode=pl.Buffered(1))


def _tile_rows(n_seq, seq_len):
    fitting = [tile for tile in TILE_CHOICES if seq_len % tile == 0]
    long_enough = [tile for tile in fitting if n_seq * (seq_len // tile) >= MIN_GRID_STEPS]
    return (long_enough or fitting[-1:])[0]


def _encoder_layer(x, weights, layer):
    bn, s, d = x.shape
    tile = _tile_rows(bn, s)
    assert d == D_MODEL and tile % CHUNK == 0 and tile % HALO == 0
    assert HALO % SUBLANES == 0 and HALO >= CONV_D_W // 2 and HALO >= max(POOL_WINDOWS) // 2
    assert tile % (PHASES * PHASE_ROWS) == 0 and tile % OUT_ROWS == 0 and N_HALF * 2 == N_GROUPS
    n_t = s // tile
    halo_per_tile = tile // HALO
    n_halo_blocks = s // HALO
    in_specs = [
        pl.BlockSpec((1, tile, d), lambda b, t: (b, t, 0)),
        pl.BlockSpec((1, HALO, d), lambda b, t: (b, jnp.maximum(t * halo_per_tile - 1, 0), 0)),
        pl.BlockSpec((1, HALO, d), lambda b, t: (b, jnp.minimum((t + 1) * halo_per_tile, n_halo_blocks - 1), 0)),
    ] + [_resident(w.shape, layer) for w in weights]
    halo_scratch = pltpu.VMEM((tile + 2 * HALO, LANES), jnp.float32)
    tile_scratch = pltpu.VMEM((tile, LANES), jnp.float32)
    return pl.pallas_call(
        functools.partial(_layer_kernel, s),
        grid=(bn, n_t),
        in_specs=in_specs,
        out_specs=pl.BlockSpec((1, tile, d), lambda b, t: (b, t, 0)),
        out_shape=jax.ShapeDtypeStruct(x.shape, x.dtype),
        scratch_shapes=[halo_scratch] * (3 * N_HALF) + [tile_scratch] * (3 * N_HALF),
        compiler_params=pltpu.CompilerParams(
            dimension_semantics=("parallel", "parallel"),
            vmem_limit_bytes=VMEM_LIMIT_BYTES),
        name="encoder_layer",
    )(x, x, x, *weights)


def _w_in_prep_kernel(src_block_ref, halve_ref, w_ref, o_ref):
    j = pl.program_id(0)
    scale = jnp.where(halve_ref[j] == 1, 0.5, 1.0)
    for k in range(o_ref.shape[2]):
        o_ref[:, 0, k] = (w_ref[:, k * MXU_TILE:(k + 1) * MXU_TILE, :] * scale).astype(o_ref.dtype)


def _prepare_w_in(w_in):
    depth, d, in_cols = w_in.shape
    n_blocks = in_cols // BR_W
    src_block = HALO_COLS + TILE_COLS + tuple(range(N_BR_COLS, n_blocks))
    halve = tuple(int(c in HALF_SCALED_COLS or c >= N_BR_COLS) for c in src_block)
    grid_spec = pltpu.PrefetchScalarGridSpec(
        num_scalar_prefetch=2,
        grid=(n_blocks,),
        in_specs=[pl.BlockSpec((depth, d, BR_W), lambda j, src, halve: (0, 0, src[j]))],
        out_specs=pl.BlockSpec((depth, 1, d // MXU_TILE, MXU_TILE, BR_W), lambda j, src, halve: (0, j, 0, 0, 0)),
    )
    return pl.pallas_call(
        _w_in_prep_kernel,
        grid_spec=grid_spec,
        out_shape=jax.ShapeDtypeStruct((depth, n_blocks, d // MXU_TILE, MXU_TILE, BR_W), jnp.bfloat16),
        compiler_params=pltpu.CompilerParams(dimension_semantics=("parallel",)),
        name="w_in_prep",
    )(jnp.array(src_block, jnp.int32), jnp.array(halve, jnp.int32), w_in)


def _prepare_weights(w_in, conv_a, ln_v_g, ln_v_b, w_s, b_s, w_pool, pool_scale,
                     conv_d, conv_d_b, ln_d_g, ln_d_b, w_pw_d, w_br, w_out, ln_g, ln_b):
    bf16 = jnp.bfloat16
    rows = lambda v: v.reshape(DEPTH, 1, -1)
    b_s_full = jnp.repeat(jnp.swapaxes(b_s, 1, 2), GROUP_W, axis=2)
    eye = jnp.eye(N_GROUPS, dtype=w_pool.dtype)
    w_pool_bd = jnp.einsum("lgij,gh->lgihj", w_pool, eye).reshape(DEPTH, BR_W, BR_W)
    return (
        _prepare_w_in(w_in), conv_a, rows(ln_v_g), rows(ln_v_b),
        w_s.reshape(DEPTH, N_GROUPS * CHUNK, CHUNK).astype(bf16), b_s_full,
        w_pool_bd.astype(bf16), rows(pool_scale), conv_d, rows(conv_d_b),
        0.5 * rows(ln_d_g), 0.5 * rows(ln_d_b),
        w_pw_d.astype(bf16), (0.5 * w_br).astype(bf16), w_out.astype(bf16), rows(ln_g), rows(ln_b),
    )


def kernel(x_prompt, x_sample, w_in, conv_a, ln_v_g, ln_v_b, w_s, b_s, w_pool, pool_scale, conv_d, conv_d_b, ln_d_g, ln_d_b, w_pw_d, w_br, w_out, ln_g, ln_b):
    weights = _prepare_weights(w_in, conv_a, ln_v_g, ln_v_b, w_s, b_s, w_pool, pool_scale, conv_d, conv_d_b,
                               ln_d_g, ln_d_b, w_pw_d, w_br, w_out, ln_g, ln_b)
    y_prompt, y_sample = x_prompt, x_sample
    for layer in range(DEPTH):
        y_prompt = _encoder_layer(y_prompt, weights, layer)
        y_sample = _encoder_layer(y_sample, weights, layer)
    return (y_prompt, y_sample)
```
